```python
import math
import jax, jax.numpy as jnp
from jax import lax
import numpy as np

D_MODEL = 1024
BATCH = 4
SEQ = 4096
DEPTH = 2
DEC_BATCH = 32
DEC_SEQ = 4
PAST_LEN = 16384
PAGE_SIZE = 128

N_MIXERS = 2
N_FOX_LAYERS = (DEPTH + 1) // 2
N_SSD_LAYERS = DEPTH // 2
FOX_HEADS = 16
FOX_HEAD_DIM = D_MODEL // FOX_HEADS
Q_BLOCK = 128
SSD_EXPAND = 2
D_INNER = SSD_EXPAND * D_MODEL
SSD_HEAD_DIM = 64
SSD_HEADS = D_INNER // SSD_HEAD_DIM
SSD_GROUPS = 8
SSD_STATE = 128
SSD_CONV = 4
SSD_CHUNK = 128
CONV_DIM = D_INNER + 2 * SSD_GROUPS * SSD_STATE
D_FF = 4 * D_MODEL
EPS = 1e-5

kernel_name = "fox_mamba2_hybrid_step"


def _rmsnorm(x, w):
    xf = x.astype(jnp.float32)
    y = xf * lax.rsqrt(jnp.mean(xf * xf, axis=-1, keepdims=True) + EPS)
    return (y * w.astype(jnp.float32)).astype(x.dtype)


def _fox_block(q, cq, qpos, segments):
    scale = FOX_HEAD_DIM ** -0.5
    cq_t = jnp.transpose(cq, (0, 2, 1))[..., :, None]
    logits = []
    for k, v, ck, kpos in segments:
        s = jnp.einsum('bqhd,bkhd->bhqk', q, k).astype(jnp.float32) * scale
        s = s + cq_t - jnp.transpose(ck, (0, 2, 1))[..., None, :]
        s = jnp.where(kpos[None, :] <= qpos[:, None], s, -jnp.inf)
        logits.append(s)
    p = jax.nn.softmax(jnp.concatenate(logits, axis=-1), axis=-1)
    sizes = [seg[0].shape[1] for seg in segments]
    parts = jnp.split(p, np.cumsum(sizes)[:-1].tolist(), axis=-1)
    outs = [jnp.einsum('bhqk,bkhd->bqhd', pp.astype(seg[1].dtype), seg[1]) for pp, seg in zip(parts, segments)]
    out = outs[0]
    for o in outs[1:]:
        out = out + o
    return out


def _fox_attention(q, cq, qpos, segments):
    B, L, H, Dh = q.shape
    blk = math.gcd(L, Q_BLOCK)
    nb = L // blk
    qb = q.reshape(B, nb, blk, H, Dh).swapaxes(0, 1)
    cqb = cq.reshape(B, nb, blk, H).swapaxes(0, 1)
    pb = qpos.reshape(nb, blk)
    out = lax.map(lambda a: _fox_block(a[0], a[1], a[2], segments), (qb, cqb, pb))
    return out.swapaxes(0, 1).reshape(B, L, H, Dh)


def _fox_mixer(xn, w_in, b_f, w_out, pos0, past):
    B, L, _ = xn.shape
    proj = xn @ w_in
    q, k, v, f = jnp.split(proj, [D_MODEL, 2 * D_MODEL, 3 * D_MODEL], axis=-1)
    q = q.reshape(B, L, FOX_HEADS, FOX_HEAD_DIM)
    k = k.reshape(B, L, FOX_HEADS, FOX_HEAD_DIM)
    v = v.reshape(B, L, FOX_HEADS, FOX_HEAD_DIM)
    logf = jax.nn.log_sigmoid((f + b_f).astype(jnp.float32))
    pos = jnp.arange(L, dtype=jnp.int32) + pos0
    if past is None:
        c = jnp.cumsum(logf, axis=1)
        segments = [(k, v, c, pos)]
    else:
        k_past, v_past, logf_past = past
        c_past = jnp.cumsum(logf_past.astype(jnp.float32), axis=1)
        c = c_past[:, -1:] + jnp.cumsum(logf, axis=1)
        kpos_past = jnp.arange(k_past.shape[1], dtype=jnp.int32)
        segments = [(k_past, v_past, c_past, kpos_past), (k, v, c, pos)]
    o = _fox_attention(q, c, pos, segments)
    y = o.reshape(B, L, D_MODEL) @ w_out
    return y, k, v, logf


def _ssd_scan(x, dt, A, Bm, Cm, h0):
    Bsz, L, H, P = x.shape
    G, R, N = SSD_GROUPS, SSD_HEADS // SSD_GROUPS, SSD_STATE
    Q = math.gcd(L, SSD_CHUNK)
    nc = L // Q
    f32 = jnp.float32
    xc = x.astype(f32).reshape(Bsz, nc, Q, G, R, P)
    dtc = dt.reshape(Bsz, nc, Q, G, R)
    Bc = Bm.astype(f32).reshape(Bsz, nc, Q, G, N)
    Cc = Cm.astype(f32).reshape(Bsz, nc, Q, G, N)
    cum = jnp.cumsum(dtc * A.reshape(G, R), axis=2)
    diff = cum[:, :, :, None] - cum[:, :, None, :]
    causal = jnp.tril(jnp.ones((Q, Q), bool))[:, :, None, None]
    decay = jnp.exp(jnp.where(causal, diff, -jnp.inf))
    cb = jnp.einsum('bcign,bcjgn->bcijg', Cc, Bc)
    w = cb[..., None] * decay * dtc[:, :, None]
    y_intra = jnp.einsum('bcijgr,bcjgrp->bcigrp', w, xc)
    wend = jnp.exp(cum[:, :, -1:] - cum) * dtc
    states = jnp.einsum('bcjgr,bcjgn,bcjgrp->bcgrpn', wend, Bc, xc)
    chunk_decay = jnp.exp(cum[:, :, -1])

    def step(h, inp):
        s, dcy = inp
        return dcy[..., None, None] * h + s, h

    h_init = h0.astype(f32).reshape(Bsz, G, R, P, N)
    h_last, h_in = lax.scan(step, h_init, (states.swapaxes(0, 1), chunk_decay.swapaxes(0, 1)))
    h_in = h_in.swapaxes(0, 1)
    y_inter = jnp.einsum('bcign,bcigr,bcgrpn->bcigrp', Cc, jnp.exp(cum), h_in)
    y = (y_intra + y_inter).reshape(Bsz, L, H, P)
    return y.astype(x.dtype), h_last.reshape(Bsz, H, P, N).astype(h0.dtype)


def _ssd_mixer(xn, w_in, conv_w, conv_b, dt_bias, a_log, d_skip, norm_w, w_out, h0, conv_buf):
    B, L, _ = xn.shape
    proj = xn @ w_in
    z, xbc, dt = jnp.split(proj, [D_INNER, D_INNER + CONV_DIM], axis=-1)
    padded = jnp.concatenate([conv_buf.astype(xbc.dtype), xbc], axis=1)
    conv = conv_b + sum(padded[:, i:i + L] * conv_w[i] for i in range(SSD_CONV))
    new_buf = padded[:, L:]
    xbc_c = jax.nn.silu(conv)
    xs, Bm, Cm = jnp.split(xbc_c, [D_INNER, D_INNER + SSD_GROUPS * SSD_STATE], axis=-1)
    xs = xs.reshape(B, L, SSD_HEADS, SSD_HEAD_DIM)
    Bm = Bm.reshape(B, L, SSD_GROUPS, SSD_STATE)
    Cm = Cm.reshape(B, L, SSD_GROUPS, SSD_STATE)
    dt = jax.nn.softplus((dt + dt_bias).astype(jnp.float32))
    A = -jnp.exp(a_log.astype(jnp.float32))
    y, h_last = _ssd_scan(xs, dt, A, Bm, Cm, h0)
    y = (y + xs * d_skip[:, None]).reshape(B, L, D_INNER)
    g = (y * jax.nn.silu(z)).astype(jnp.float32).reshape(B, L, SSD_GROUPS, D_INNER // SSD_GROUPS)
    g = g * lax.rsqrt(jnp.mean(g * g, axis=-1, keepdims=True) + EPS)
    g = (g.reshape(B, L, D_INNER) * norm_w.astype(jnp.float32)).astype(xn.dtype)
    return g @ w_out, h_last, new_buf


def _mlp(xn, w_up, w_down):
    h = jax.nn.relu(xn @ w_up)
    return (h * h) @ w_down


def _trunk(x, pos0, fox_past, ssm0, conv0, weights):
    (norm_mix_w, norm_mlp_w, norm_out_w, fox_w_in, fox_b_f, fox_w_out,
     ssd_w_in, ssd_conv_w, ssd_conv_b, ssd_dt_bias, ssd_a_log, ssd_d, ssd_norm_w, ssd_w_out,
     mlp_w_up, mlp_w_down) = weights
    k_rows, v_rows, f_rows, ssm_out, conv_out = [], [], [], [], []
    for i in range(DEPTH):
        xn = _rmsnorm(x, norm_mix_w[i])
        j = i // N_MIXERS
        if i % N_MIXERS == 0:
            past = None if fox_past is None else fox_past[j]
            y, k, v, lf = _fox_mixer(xn, fox_w_in[j], fox_b_f[j], fox_w_out[j], pos0, past)
            k_rows.append(k)
            v_rows.append(v)
            f_rows.append(lf)
        else:
            y, h, cbuf = _ssd_mixer(xn, ssd_w_in[j], ssd_conv_w[j], ssd_conv_b[j], ssd_dt_bias[j],
                                    ssd_a_log[j], ssd_d[j], ssd_norm_w[j], ssd_w_out[j],
                                    ssm0[:, j], conv0[:, j])
            ssm_out.append(h)
            conv_out.append(cbuf)
        x = x + y
        x = x + _mlp(_rmsnorm(x, norm_mlp_w[i]), mlp_w_up[i], mlp_w_down[i])
    y_out = _rmsnorm(x, norm_out_w)
    return (y_out, jnp.stack(k_rows, axis=1), jnp.stack(v_rows, axis=1), jnp.stack(f_rows, axis=1),
            jnp.stack(ssm_out, axis=1), jnp.stack(conv_out, axis=1))


def setup_inputs(seed: int = 0) -> dict:
    key = jax.random.key(seed)
    ks = jax.random.split(key, 32)
    n_pages = PAST_LEN // PAGE_SIZE
    n_pool = (DEC_BATCH * n_pages * 5) // 4
    f32 = jnp.float32

    def nrm(k, shape, scale=1.0):
        return jax.random.normal(k, shape, f32) * scale

    x_prompt = nrm(ks[0], (BATCH, SEQ, D_MODEL))
    x_sample = nrm(ks[1], (DEC_BATCH, DEC_SEQ, D_MODEL))
    cache_k = nrm(ks[2], (n_pool, N_FOX_LAYERS, PAGE_SIZE, FOX_HEADS, FOX_HEAD_DIM))
    cache_v = nrm(ks[3], (n_pool, N_FOX_LAYERS, PAGE_SIZE, FOX_HEADS, FOX_HEAD_DIM))
    cache_logf = jax.nn.log_sigmoid(nrm(ks[4], (n_pool, N_FOX_LAYERS, PAGE_SIZE, FOX_HEADS)) + 4.0)
    page_table = jax.random.permutation(ks[5], n_pool)[:DEC_BATCH * n_pages].reshape(DEC_BATCH, n_pages).astype(jnp.int32)
    state_ssm = nrm(ks[6], (DEC_BATCH, N_SSD_LAYERS, SSD_HEADS, SSD_HEAD_DIM, SSD_STATE), 0.1)
    state_conv = nrm(ks[7], (DEC_BATCH, N_SSD_LAYERS, SSD_CONV - 1, CONV_DIM))

    norm_mix_w = 1.0 + nrm(ks[8], (DEPTH, D_MODEL), 0.02)
    norm_mlp_w = 1.0 + nrm(ks[9], (DEPTH, D_MODEL), 0.02)
    norm_out_w = 1.0 + nrm(ks[10], (D_MODEL,), 0.02)
    fox_w_in = nrm(ks[11], (N_FOX_LAYERS, D_MODEL, 3 * D_MODEL + FOX_HEADS), D_MODEL ** -0.5)
    fox_b_f = jax.random.uniform(ks[12], (N_FOX_LAYERS, FOX_HEADS), f32, 1.0, 6.0)
    fox_w_out = nrm(ks[13], (N_FOX_LAYERS, D_MODEL, D_MODEL), D_MODEL ** -0.5)
    ssd_w_in = nrm(ks[14], (N_SSD_LAYERS, D_MODEL, 2 * D_INNER + 2 * SSD_GROUPS * SSD_STATE + SSD_HEADS), D_MODEL ** -0.5)
    ssd_conv_w = nrm(ks[15], (N_SSD_LAYERS, SSD_CONV, CONV_DIM), SSD_CONV ** -0.5)
    ssd_conv_b = nrm(ks[16], (N_SSD_LAYERS, CONV_DIM), 0.02)
    dt0 = jnp.exp(jax.random.uniform(ks[17], (N_SSD_LAYERS, SSD_HEADS), f32, math.log(1e-3), math.log(1e-1)))
    ssd_dt_bias = dt0 + jnp.log(-jnp.expm1(-dt0))
    ssd_a_log = jnp.log(jax.random.uniform(ks[18], (N_SSD_LAYERS, SSD_HEADS), f32, 1.0, 16.0))
    ssd_d = 1.0 + nrm(ks[19], (N_SSD_LAYERS, SSD_HEADS), 0.02)
    ssd_norm_w = 1.0 + nrm(ks[20], (N_SSD_LAYERS, D_INNER), 0.02)
    ssd_w_out = nrm(ks[21], (N_SSD_LAYERS, D_INNER, D_MODEL), D_INNER ** -0.5)
    mlp_w_up = nrm(ks[22], (DEPTH, D_MODEL, D_FF), D_MODEL ** -0.5)
    mlp_w_down = nrm(ks[23], (DEPTH, D_FF, D_MODEL), D_FF ** -0.5)
    return {"x_prompt": x_prompt, "x_sample": x_sample, "cache_k": cache_k, "cache_v": cache_v,
            "cache_logf": cache_logf, "page_table": page_table, "state_ssm": state_ssm,
            "state_conv": state_conv, "norm_mix_w": norm_mix_w, "norm_mlp_w": norm_mlp_w,
            "norm_out_w": norm_out_w, "fox_w_in": fox_w_in, "fox_b_f": fox_b_f, "fox_w_out": fox_w_out,
            "ssd_w_in": ssd_w_in, "ssd_conv_w": ssd_conv_w, "ssd_conv_b": ssd_conv_b,
            "ssd_dt_bias": ssd_dt_bias, "ssd_a_log": ssd_a_log, "ssd_d": ssd_d, "ssd_norm_w": ssd_norm_w,
            "ssd_w_out": ssd_w_out, "mlp_w_up": mlp_w_up, "mlp_w_down": mlp_w_down}


def reference(x_prompt, x_sample, cache_k, cache_v, cache_logf, page_table, state_ssm, state_conv,
              norm_mix_w, norm_mlp_w, norm_out_w, fox_w_in, fox_b_f, fox_w_out,
              ssd_w_in, ssd_conv_w, ssd_conv_b, ssd_dt_bias, ssd_a_log, ssd_d, ssd_norm_w, ssd_w_out,
              mlp_w_up, mlp_w_down):
    weights = (norm_mix_w, norm_mlp_w, norm_out_w, fox_w_in, fox_b_f, fox_w_out,
               ssd_w_in, ssd_conv_w, ssd_conv_b, ssd_dt_bias, ssd_a_log, ssd_d, ssd_norm_w, ssd_w_out,
               mlp_w_up, mlp_w_down)
    b_p = x_prompt.shape[0]
    ssm0_p = jnp.zeros((b_p, N_SSD_LAYERS, SSD_HEADS, SSD_HEAD_DIM, SSD_STATE), state_ssm.dtype)
    conv0_p = jnp.zeros((b_p, N_SSD_LAYERS, SSD_CONV - 1, CONV_DIM), state_conv.dtype)
    (y_prompt, new_k_prompt, new_v_prompt, new_logf_prompt, new_ssm_prompt,
     new_conv_prompt) = _trunk(x_prompt, 0, None, ssm0_p, conv0_p, weights)

    b_s = x_sample.shape[0]
    past_len = page_table.shape[1] * cache_k.shape[2]
    fox_past = []
    for j in range(N_FOX_LAYERS):
        k_past = cache_k[page_table, j].reshape(b_s, past_len, FOX_HEADS, FOX_HEAD_DIM)
        v_past = cache_v[page_table, j].reshape(b_s, past_len, FOX_HEADS, FOX_HEAD_DIM)
        f_past = cache_logf[page_table, j].reshape(b_s, past_len, FOX_HEADS)
        fox_past.append((k_past, v_past, f_past))
    (y_sample, new_k_sample, new_v_sample, new_logf_sample, new_ssm_sample,
     new_conv_sample) = _trunk(x_sample, past_len, fox_past, state_ssm, state_conv, weights)
    return (y_prompt, y_sample, new_k_prompt, new_v_prompt, new_logf_prompt, new_ssm_prompt,
            new_conv_prompt, new_k_sample, new_v_sample, new_logf_sample, new_ssm_sample,
            new_conv_sample)
```

```python
import functools

import jax
import jax.numpy as jnp
from jax import lax
from jax.experimental import pallas as pl
from jax.experimental.pallas import tpu as pltpu

F32 = jnp.float32
BF16 = jnp.bfloat16
EPS = 1e-5
LANES = 128
HEAD_DIM = 64
SSD_STATE = 128
SSD_GROUP_W = 256
VMEM_LIMIT = 52 * 1024 * 1024

NT = (((1,), (1,)), ((), ()))
NN = (((1,), (0,)), ((), ()))
TN = (((0,), (0,)), ((), ()))


def _cparams(sem):
    return pltpu.CompilerParams(dimension_semantics=sem, vmem_limit_bytes=VMEM_LIMIT)


def _dot(a, b, dims=NN):
    return lax.dot_general(a, b, dims, preferred_element_type=F32)


def _split3(x):
    hi = x.astype(BF16)
    r = x - hi.astype(F32)
    mid = r.astype(BF16)
    lo = (r - mid.astype(F32)).astype(BF16)
    return hi, mid, lo


def _dot_f32_lhs(x, sel, dims=NN):
    hi, mid, lo = _split3(x)
    return _dot(hi, sel, dims) + _dot(mid, sel, dims) + _dot(lo, sel, dims)


def _dot_f32_rhs(sel, x, dims=NN):
    hi, mid, lo = _split3(x)
    return _dot(sel, hi, dims) + _dot(sel, mid, dims) + _dot(sel, lo, dims)


def _rmsnorm(x, w):
    ms = jnp.mean(x * x, axis=-1, keepdims=True)
    return x * lax.rsqrt(ms + EPS) * w


def _softplus(x):
    return jnp.maximum(x, 0.0) + jnp.log1p(jnp.exp(-jnp.abs(x)))


def _log_sigmoid(x):
    return -_softplus(-x)


def _silu(x):
    return x * (1.0 / (1.0 + jnp.exp(-x)))


def _fox_in_prompt_kernel(x_ref, nw_ref, wq_ref, wk_ref, wv_ref, wf_ref, bf_ref,
                          q_ref, kt_ref, vt_ref, ktb_ref, vtb_ref, lft_ref, *, scale, nchunk):
    xn = _rmsnorm(x_ref[...], nw_ref[...]).astype(BF16)
    d = xn.shape[1]
    cw = d // nchunk
    for c in range(nchunk):
        sl = slice(c * cw, (c + 1) * cw)
        q = _dot(xn, wq_ref[sl, :], NT)
        q_ref[:, sl] = (q * scale).astype(BF16)
        kt = _dot(wk_ref[sl, :], xn, NT)
        kt_ref[0, sl, :] = kt
        ktb_ref[0, sl, :] = kt.astype(BF16)
        vt = _dot(wv_ref[sl, :], xn, NT)
        vt_ref[0, sl, :] = vt
        vtb_ref[0, sl, :] = vt.astype(BF16)
    ft = _dot(wf_ref[...], xn, NT)
    lft_ref[0] = _log_sigmoid(ft + bf_ref[...])


def _fox_in_prompt(x, nw, wq, wk, wv, wf, bf_col, *, batch, seq, tm=512):
    t, d = x.shape
    nh = wf.shape[0]
    nt = seq // tm
    scale = HEAD_DIM ** -0.5
    row = lambda i: (i, 0)
    full = lambda i: (0, 0)
    tr = lambda i: (i // nt, 0, i % nt)
    kern = functools.partial(_fox_in_prompt_kernel, scale=scale, nchunk=4)
    return pl.pallas_call(
        kern,
        grid=(t // tm,),
        in_specs=[pl.BlockSpec((tm, d), row), pl.BlockSpec((1, d), full),
                  pl.BlockSpec((d, d), full), pl.BlockSpec((d, d), full), pl.BlockSpec((d, d), full),
                  pl.BlockSpec((nh, d), full), pl.BlockSpec((nh, 1), full)],
        out_specs=[pl.BlockSpec((tm, d), row),
                   pl.BlockSpec((1, d, tm), tr), pl.BlockSpec((1, d, tm), tr),
                   pl.BlockSpec((1, d, tm), tr), pl.BlockSpec((1, d, tm), tr),
                   pl.BlockSpec((1, nh, tm), tr)],
        out_shape=[jax.ShapeDtypeStruct((t, d), BF16),
                   jax.ShapeDtypeStruct((batch, d, seq), F32), jax.ShapeDtypeStruct((batch, d, seq), F32),
                   jax.ShapeDtypeStruct((batch, d, seq), BF16), jax.ShapeDtypeStruct((batch, d, seq), BF16),
                   jax.ShapeDtypeStruct((batch, nh, seq), F32)],
        compiler_params=_cparams(("parallel",)),
        name="fox_in_prompt",
    )(x, nw, wq, wk, wv, wf, bf_col)


def _cumsum_kernel(lf_ref, u_ref, c_ref, *, chunk):
    nh, seq = lf_ref.shape[1], lf_ref.shape[2]
    carry = jnp.zeros((nh, 1), F32)
    for j in range(seq // chunk):
        sl = slice(j * chunk, (j + 1) * chunk)
        c = _dot_f32_lhs(lf_ref[0, :, sl], u_ref[...]) + carry
        c_ref[0, :, sl] = c
        carry = c[:, chunk - 1:chunk]


def _cumsum_lanes(lft, utri):
    b, nh, seq = lft.shape
    chunk = utri.shape[0]
    return pl.pallas_call(
        functools.partial(_cumsum_kernel, chunk=chunk),
        grid=(b,),
        in_specs=[pl.BlockSpec((1, nh, seq), lambda i: (i, 0, 0)),
                  pl.BlockSpec((chunk, chunk), lambda i: (0, 0))],
        out_specs=pl.BlockSpec((1, nh, seq), lambda i: (i, 0, 0)),
        out_shape=jax.ShapeDtypeStruct((b, nh, seq), F32),
        compiler_params=_cparams(("parallel",)),
        name="fox_cumsum",
    )(lft, utri)


def _attn_prompt_kernel(q_ref, kt_ref, vt_ref, c_ref, o_ref, m_sc, l_sc, acc_sc, *, blk):
    i = pl.program_id(2)
    q2 = q_ref[...]
    lane = lax.broadcasted_iota(jnp.int32, (blk, LANES), 1)
    lo = lane < HEAD_DIM
    zero = jnp.zeros_like(q2)
    qs = jnp.concatenate([jnp.where(lo, q2, zero), jnp.where(lo, zero, q2)], axis=0)
    m_sc[...] = jnp.full(m_sc.shape, -jnp.inf, F32)
    l_sc[...] = jnp.zeros(l_sc.shape, F32)
    acc_sc[...] = jnp.zeros(acc_sc.shape, F32)

    def step(j, masked):
        off = pl.multiple_of(j * blk, blk)
        kt = kt_ref[0, :, pl.ds(off, blk)]
        vt = vt_ref[0, :, pl.ds(off, blk)]
        ck = c_ref[0, 0, :, pl.ds(off, blk)]
        s = _dot(qs, kt)
        s0 = s[:blk] - ck[0:1, :]
        s1 = s[blk:] - ck[1:2, :]
        if masked:
            r = lax.broadcasted_iota(jnp.int32, (blk, blk), 0)
            cidx = lax.broadcasted_iota(jnp.int32, (blk, blk), 1)
            vis = cidx <= r
            s0 = jnp.where(vis, s0, -jnp.inf)
            s1 = jnp.where(vis, s1, -jnp.inf)
        s = jnp.concatenate([s0, s1], axis=0)
        m_prev = m_sc[...]
        m_new = jnp.maximum(m_prev, jnp.max(s, axis=-1, keepdims=True))
        alpha = jnp.exp(m_prev - m_new)
        p = jnp.exp(s - m_new)
        l_sc[...] = alpha * l_sc[...] + jnp.sum(p, axis=-1, keepdims=True)
        acc_sc[...] = alpha * acc_sc[...] + _dot(p.astype(BF16), vt, NT)
        m_sc[...] = m_new

    def body(j, carry):
        step(j, False)
        return carry

    lax.fori_loop(0, i, body, 0)
    step(i, True)
    out = acc_sc[...] / l_sc[...]
    o_ref[...] = jnp.where(lo, out[:blk], out[blk:]).astype(o_ref.dtype)


def _attn_prompt(q, ktb, vtb, c4, *, batch, seq, blk=512):
    t, d = q.shape
    npair = d // LANES
    nq = seq // blk
    qmap = lambda b, h, i: (b * nq + i, h)
    kmap = lambda b, h, i: (b, h, 0)
    return pl.pallas_call(
        functools.partial(_attn_prompt_kernel, blk=blk),
        grid=(batch, npair, nq),
        in_specs=[pl.BlockSpec((blk, LANES), qmap),
                  pl.BlockSpec((1, LANES, seq), kmap), pl.BlockSpec((1, LANES, seq), kmap),
                  pl.BlockSpec((1, 1, 2, seq), lambda b, h, i: (b, h, 0, 0))],
        out_specs=pl.BlockSpec((blk, LANES), qmap),
        out_shape=jax.ShapeDtypeStruct((t, d), BF16),
        scratch_shapes=[pltpu.VMEM((2 * blk, 1), F32), pltpu.VMEM((2 * blk, 1), F32),
                        pltpu.VMEM((2 * blk, LANES), F32)],
        compiler_params=_cparams(("parallel", "parallel", "arbitrary")),
        name="fox_attn_prompt",
    )(q, ktb, vtb, c4)


def _linear_residual_kernel(a_ref, w_ref, x_ref, o_ref):
    o_ref[...] = x_ref[...] + _dot(a_ref[...].astype(BF16), w_ref[...])


def _linear_residual(a, w, x, *, tm=512):
    t, k = a.shape
    n = w.shape[1]
    tm = min(tm, t)
    return pl.pallas_call(
        _linear_residual_kernel,
        grid=(t // tm,),
        in_specs=[pl.BlockSpec((tm, k), lambda i: (i, 0)), pl.BlockSpec((k, n), lambda i: (0, 0)),
                  pl.BlockSpec((tm, n), lambda i: (i, 0))],
        out_specs=pl.BlockSpec((tm, n), lambda i: (i, 0)),
        out_shape=jax.ShapeDtypeStruct((t, n), F32),
        compiler_params=_cparams(("parallel",)),
        name="linear_residual",
    )(a, w, x)


def _mlp_kernel(x_ref, nw_ref, wup_ref, wdn_ref, fw_ref, o_ref, xn_sc, acc_sc, *, final_norm):
    f = pl.program_id(1)

    @pl.when(f == 0)
    def _():
        x = x_ref[...]
        xn_sc[...] = _rmsnorm(x, nw_ref[...]).astype(BF16)
        acc_sc[...] = x

    h = jnp.maximum(_dot(xn_sc[...], wup_ref[...]), 0.0)
    acc_sc[...] += _dot((h * h).astype(BF16), wdn_ref[...])

    @pl.when(f == pl.num_programs(1) - 1)
    def _():
        y = acc_sc[...]
        if final_norm:
            y = _rmsnorm(y, fw_ref[...])
        o_ref[...] = y


def _mlp_block(x, nw, wup, wdn, fw, *, final_norm, tm=512, tf=1024):
    t, d = x.shape
    dff = wup.shape[1]
    tm = min(tm, t)
    return pl.pallas_call(
        functools.partial(_mlp_kernel, final_norm=final_norm),
        grid=(t // tm, dff // tf),
        in_specs=[pl.BlockSpec((tm, d), lambda i, f: (i, 0)), pl.BlockSpec((1, d), lambda i, f: (0, 0)),
                  pl.BlockSpec((d, tf), lambda i, f: (0, f)), pl.BlockSpec((tf, d), lambda i, f: (f, 0)),
                  pl.BlockSpec((1, d), lambda i, f: (0, 0))],
        out_specs=pl.BlockSpec((tm, d), lambda i, f: (i, 0)),
        out_shape=jax.ShapeDtypeStruct((t, d), F32),
        scratch_shapes=[pltpu.VMEM((tm, d), BF16), pltpu.VMEM((tm, d), F32)],
        compiler_params=_cparams(("parallel", "arbitrary")),
        name="mlp_block",
    )(x, nw, wup, wdn, fw)


def _norm_linear_kernel(x_ref, nw_ref, wt_ref, o_ref):
    xn = _rmsnorm(x_ref[...], nw_ref[...]).astype(BF16)
    o_ref[...] = _dot(xn, wt_ref[...], NT)


def _norm_linear(x, nw, wt, *, tm=128):
    t, d = x.shape
    n = wt.shape[0]
    return pl.pallas_call(
        _norm_linear_kernel,
        grid=(t // tm,),
        in_specs=[pl.BlockSpec((tm, d), lambda i: (i, 0)), pl.BlockSpec((1, d), lambda i: (0, 0)),
                  pl.BlockSpec((n, d), lambda i: (0, 0))],
        out_specs=pl.BlockSpec((tm, n), lambda i: (i, 0)),
        out_shape=jax.ShapeDtypeStruct((t, n), F32),
        compiler_params=_cparams(("parallel",)),
        name="norm_linear",
    )(x, nw, wt)


def _ssd_in_kernel(x_ref, nw_ref, w_ref, wdtp_ref, wdt_ref, brow_ref, bcol_ref,
                   zx_ref, dtn_ref, dtt_ref, xn_sc):
    j = pl.program_id(1)

    @pl.when(j == 0)
    def _():
        xn = _rmsnorm(x_ref[...], nw_ref[...]).astype(BF16)
        xn_sc[...] = xn
        dtn_ref[...] = _softplus(_dot(xn, wdtp_ref[...], NT) + brow_ref[...])
        dtt_ref[...] = _softplus(_dot(wdt_ref[...], xn, NT) + bcol_ref[...])

    zx_ref[...] = _dot(xn_sc[...], w_ref[...], NT)


def _ssd_in(x, nw, wzx, wdtp, wdt, brow, bcol, *, tm=1024, tn=1024):
    t, d = x.shape
    n = wzx.shape[0]
    nh = wdt.shape[0]
    tm = min(tm, t)
    return pl.pallas_call(
        _ssd_in_kernel,
        grid=(t // tm, n // tn),
        in_specs=[pl.BlockSpec((tm, d), lambda i, j: (i, 0)), pl.BlockSpec((1, d), lambda i, j: (0, 0)),
                  pl.BlockSpec((tn, d), lambda i, j: (j, 0)),
                  pl.BlockSpec((LANES, d), lambda i, j: (0, 0)), pl.BlockSpec((nh, d), lambda i, j: (0, 0)),
                  pl.BlockSpec((1, LANES), lambda i, j: (0, 0)), pl.BlockSpec((nh, 1), lambda i, j: (0, 0))],
        out_specs=[pl.BlockSpec((tm, tn), lambda i, j: (i, j)),
                   pl.BlockSpec((tm, LANES), lambda i, j: (i, 0)),
                   pl.BlockSpec((nh, tm), lambda i, j: (0, i))],
        out_shape=[jax.ShapeDtypeStruct((t, n), F32), jax.ShapeDtypeStruct((t, LANES), F32),
                   jax.ShapeDtypeStruct((nh, t), F32)],
        scratch_shapes=[pltpu.VMEM((tm, d), BF16)],
        compiler_params=_cparams(("parallel", "arbitrary")),
        name="ssd_in",
    )(x, nw, wzx, wdtp, wdt, brow, bcol)


def _ssd_scan_kernel(z_ref, xs_ref, bc_ref, dtn_ref, dtt_ref, h0_ref, tx_ref, tbc_ref,
                     cwx_ref, cwbc_ref, cbx_ref, cbbc_ref, arow_ref, acol_ref, drow_ref, nw_ref,
                     ltri_ref, utri_ref,
                     gy_ref, hout_ref,
                     h_sc, xpad_sc, bcpad_sc, dtn_sc, dtt_sc, z_sc, *, q, valid, nheads):
    c = pl.program_id(1)
    ngroups = xs_ref.shape[-1] // SSD_GROUP_W

    @pl.when(c == 0)
    def _():
        h_sc[...] = h0_ref[...]
        xpad_sc[0:8, :] = tx_ref[...]
        bcpad_sc[0:8, :] = tbc_ref[...]

    if valid == q:
        xpad_sc[8:8 + q, :] = xs_ref[...]
        bcpad_sc[8:8 + q, :] = bc_ref[...]
        dtn = dtn_ref[...]
        dtt = dtt_ref[...]
        z_src = z_ref
    else:
        xpad_sc[8:8 + q, :] = jnp.zeros((q, xpad_sc.shape[1]), F32)
        bcpad_sc[8:8 + q, :] = jnp.zeros((q, bcpad_sc.shape[1]), F32)
        xpad_sc[8:8 + valid, :] = xs_ref[...]
        bcpad_sc[8:8 + valid, :] = bc_ref[...]
        dtn_sc[...] = jnp.zeros(dtn_sc.shape, F32)
        dtn_sc[0:valid, :] = dtn_ref[...]
        dtt_sc[...] = jnp.zeros(dtt_sc.shape, F32)
        dtt_sc[:, 0:valid] = dtt_ref[...]
        z_sc[...] = jnp.zeros(z_sc.shape, F32)
        z_sc[0:valid, :] = z_ref[...]
        dtn = dtn_sc[...]
        dtt = dtt_sc[...]
        z_src = z_sc

    def conv(pad_sc, w_ref, b_ref, sl):
        w = w_ref[:, sl]
        acc = b_ref[:, sl] + pad_sc[5:5 + q, sl] * w[0:1, :]
        acc = acc + pad_sc[6:6 + q, sl] * w[1:2, :]
        acc = acc + pad_sc[7:7 + q, sl] * w[2:3, :]
        acc = acc + pad_sc[8:8 + q, sl] * w[3:4, :]
        return _silu(acc)

    a_nat = dtn * arow_ref[...]
    a_t = dtt * acol_ref[...]
    cum_nat = _dot_f32_rhs(ltri_ref[...], a_nat)
    cum_t = _dot_f32_lhs(a_t, utri_ref[...])
    ri = lax.broadcasted_iota(jnp.int32, (q, q), 0)
    ci = lax.broadcasted_iota(jnp.int32, (q, q), 1)
    tri = ci <= ri
    lo = lax.broadcasted_iota(jnp.int32, (q, LANES), 1) < HEAD_DIM
    rlo = lax.broadcasted_iota(jnp.int32, (LANES, SSD_STATE), 0) < HEAD_DIM
    nb = bc_ref.shape[-1] // 2

    for g in range(ngroups):
        bg = conv(bcpad_sc, cwbc_ref, cbbc_ref, slice(g * SSD_STATE, (g + 1) * SSD_STATE))
        cg = conv(bcpad_sc, cwbc_ref, cbbc_ref, slice(nb + g * SSD_STATE, nb + (g + 1) * SSD_STATE))
        bb = bg.astype(BF16)
        cb16 = cg.astype(BF16)
        cb = _dot(cb16, bb, NT)
        gated = []
        ssq = jnp.zeros((q, 1), F32)
        for pr in range(2):
            p = 2 * g + pr
            sl = slice(p * LANES, (p + 1) * LANES)
            xp = conv(xpad_sc, cwx_ref, cbx_ref, sl)
            xpb = xp.astype(BF16)
            yi, ee, we, dl = [], [], [], []
            for r in range(2):
                h = 2 * p + r
                colb = jnp.broadcast_to(cum_nat[:, h:h + 1], (q, LANES))
                rowb = cum_t[h:h + 1, :]
                dec = jnp.exp(jnp.where(tri, colb - rowb, -jnp.inf))
                w = (cb * dec * dtt[h:h + 1, :]).astype(BF16)
                yi.append(_dot(w, xpb))
                ee.append(jnp.exp(colb))
                clast = cum_nat[q - 1:q, h:h + 1]
                we.append(jnp.exp(clast - colb) * jnp.broadcast_to(dtn[:, h:h + 1], (q, LANES)))
                dl.append(jnp.broadcast_to(jnp.exp(clast), (LANES, SSD_STATE)))
            hp = h_sc[sl, :]
            y_inter = _dot(cb16, hp.astype(BF16), NT) * jnp.where(lo, ee[0], ee[1])
            xw = (xp * jnp.where(lo, we[0], we[1])).astype(BF16)
            h_sc[sl, :] = jnp.where(rlo, dl[0], dl[1]) * hp + _dot(xw, bb, TN)
            y = jnp.where(lo, yi[0], yi[1]) + y_inter + xp * drow_ref[:, sl]
            gt = y * _silu(z_src[:, sl])
            ssq = ssq + jnp.sum(gt * gt, axis=-1, keepdims=True)
            gated.append(gt)
        rs = lax.rsqrt(ssq / SSD_GROUP_W + EPS)
        for pr in range(2):
            sl = slice((2 * g + pr) * LANES, (2 * g + pr + 1) * LANES)
            gy_ref[:, sl] = (gated[pr] * rs * nw_ref[:, sl])[0:valid].astype(gy_ref.dtype)

    xpad_sc[0:8, :] = xpad_sc[q:q + 8, :]
    bcpad_sc[0:8, :] = bcpad_sc[q:q + 8, :]

    @pl.when(c == pl.num_programs(1) - 1)
    def _():
        hout_ref[...] = h_sc[...]


def _ssd_scan(zx3, dtn3, dtt3, h0, tail, cw, cb, arow, acol, drow, nw, ltri, utri, *, q=128):
    b, seq, _ = zx3.shape
    nheads = dtt3.shape[1]
    di = nheads * HEAD_DIM
    valid = min(q, seq)
    nc = max(seq // q, 1)
    hrows = nheads * HEAD_DIM
    col = lambda k: (lambda i, c: (i, c, k))
    const2 = lambda k: (lambda i, c: (0, k))
    kern = functools.partial(_ssd_scan_kernel, q=q, valid=valid, nheads=nheads)
    return pl.pallas_call(
        kern,
        grid=(b, nc),
        in_specs=[pl.BlockSpec((None, valid, di), col(0)), pl.BlockSpec((None, valid, di), col(1)),
                  pl.BlockSpec((None, valid, di), col(2)),
                  pl.BlockSpec((None, valid, LANES), lambda i, c: (i, c, 0)),
                  pl.BlockSpec((None, nheads, valid), lambda i, c: (i, 0, c)),
                  pl.BlockSpec((None, hrows, SSD_STATE), lambda i, c: (i, 0, 0)),
                  pl.BlockSpec((None, 8, di), lambda i, c: (i, 0, 0)),
                  pl.BlockSpec((None, 8, di), lambda i, c: (i, 0, 1)),
                  pl.BlockSpec((4, di), const2(0)), pl.BlockSpec((4, di), const2(1)),
                  pl.BlockSpec((1, di), const2(0)), pl.BlockSpec((1, di), const2(1)),
                  pl.BlockSpec((1, LANES), const2(0)), pl.BlockSpec((nheads, 1), const2(0)),
                  pl.BlockSpec((1, di), const2(0)), pl.BlockSpec((1, di), const2(0)),
                  pl.BlockSpec((q, q), const2(0)), pl.BlockSpec((q, q), const2(0))],
        out_specs=[pl.BlockSpec((None, valid, di), lambda i, c: (i, c, 0)),
                   pl.BlockSpec((None, hrows, SSD_STATE), lambda i, c: (i, 0, 0))],
        out_shape=[jax.ShapeDtypeStruct((b, seq, di), BF16 if valid == q else F32),
                   jax.ShapeDtypeStruct((b, hrows, SSD_STATE), F32)],
        scratch_shapes=[pltpu.VMEM((hrows, SSD_STATE), F32),
                        pltpu.VMEM((q + 8, di), F32), pltpu.VMEM((q + 8, di), F32),
                        pltpu.VMEM((q, LANES), F32), pltpu.VMEM((nheads, q), F32),
                        pltpu.VMEM((q, di), F32)],
        compiler_params=_cparams(("parallel", "arbitrary")),
        name="ssd_scan",
    )(zx3, zx3, zx3, dtn3, dtt3, h0, tail, tail, cw, cw, cb, cb, arow, acol, drow, nw, ltri, utri)


def _attn_sample_kernel(pt_ref, q_ref, kn_ref, vn_ref, f_ref, ft_ref, bfr_ref, bfc_ref, u_ref, hm_ref,
                        *refs, pg, nq, nh, scale):
    k_refs, v_refs, lf_refs = refs[0:pg], refs[pg:2 * pg], refs[2 * pg:3 * pg]
    o_ref, lfo_ref = refs[3 * pg], refs[3 * pg + 1]
    qbd_sc, m_sc, l_sc, acc_sc, carry_sc, knp_sc, vnp_sc = refs[3 * pg + 2:]
    g = pl.program_id(1)
    rows = nq * nh

    @pl.when(g == 0)
    def _():
        qv = q_ref[...] * scale
        hm = hm_ref[...]
        for qi in range(nq):
            qbd_sc[qi * nh:(qi + 1) * nh, :] = (qv[qi:qi + 1, :] * hm).astype(BF16)
        m_sc[...] = jnp.full(m_sc.shape, -jnp.inf, F32)
        l_sc[...] = jnp.zeros(l_sc.shape, F32)
        acc_sc[...] = jnp.zeros(acc_sc.shape, F32)
        carry_sc[...] = jnp.zeros(carry_sc.shape, F32)

    def update(s, pv_fn):
        m_prev = m_sc[...]
        m_new = jnp.maximum(m_prev, jnp.max(s, axis=-1, keepdims=True))
        alpha = jnp.exp(m_prev - m_new)
        p = jnp.exp(s - m_new)
        l_sc[...] = alpha * l_sc[...] + jnp.sum(p, axis=-1, keepdims=True)
        acc_sc[...] = alpha * acc_sc[...] + pv_fn(p.astype(BF16))
        m_sc[...] = m_new

    for pi in range(pg):
        cpage = _dot_f32_lhs(lf_refs[pi][...], u_ref[...]) + carry_sc[...]
        carry_sc[...] = cpage[:, LANES - 1:LANES]
        kt = k_refs[pi][...].astype(BF16)
        s = _dot(qbd_sc[...], kt) - jnp.concatenate([cpage] * nq, axis=0)
        vt = v_refs[pi][...].astype(BF16)
        update(s, lambda p, vt=vt: _dot(p, vt, NT))

    @pl.when(g == pl.num_programs(1) - 1)
    def _():
        lfo_ref[...] = _log_sigmoid(f_ref[:, 0:nh] + bfr_ref[...])
        lft = _log_sigmoid(ft_ref[...] + bfc_ref[...])
        lane = lax.broadcasted_iota(jnp.int32, (nh, LANES), 1)
        cnew = jnp.zeros((nh, LANES), F32)
        run = carry_sc[...]
        for t in range(nq):
            run = run + lft[:, t:t + 1]
            cnew = jnp.where(lane == t, run, cnew)
        knp_sc[...] = jnp.zeros(knp_sc.shape, F32)
        knp_sc[0:nq, :] = kn_ref[...]
        vnp_sc[...] = jnp.zeros(vnp_sc.shape, F32)
        vnp_sc[0:nq, :] = vn_ref[...]
        s = _dot(qbd_sc[...], knp_sc[...].astype(BF16), NT) - jnp.concatenate([cnew] * nq, axis=0)
        key = lax.broadcasted_iota(jnp.int32, (rows, LANES), 1)
        ridx = lax.broadcasted_iota(jnp.int32, (rows, LANES), 0)
        qrow = jnp.zeros((rows, LANES), jnp.int32)
        for qi in range(1, nq):
            qrow = qrow + (ridx >= qi * nh).astype(jnp.int32)
        s = jnp.where(key <= qrow, s, -jnp.inf)
        vnb = vnp_sc[...].astype(BF16)
        update(s, lambda p: _dot(p, vnb))
        out = acc_sc[...] / l_sc[...]
        hm = hm_ref[...]
        for qi in range(nq):
            o_ref[qi:qi + 1, :] = jnp.sum(out[qi * nh:(qi + 1) * nh, :] * hm, axis=0,
                                          keepdims=True).astype(o_ref.dtype)


def _attn_sample(page_table, proj3, ft3, bf_row, bf_col, utri, hmask, kt_cache, vt_cache, lf_cache,
                 *, d, pg=4):
    nseq, nq, _ = proj3.shape
    npages = page_table.shape[1]
    nh = hmask.shape[0]
    rows = nq * nh
    scale = HEAD_DIM ** -0.5
    page = kt_cache.shape[-1]
    fblk = (3 * d) // LANES

    def pmap(pi):
        return lambda b, g, pt: (pt[b, g * pg + pi], 0, 0, 0)

    cmap = lambda b, g, pt: (0, 0)
    in_specs = [pl.BlockSpec((None, nq, d), lambda b, g, pt: (b, 0, 0)),
                pl.BlockSpec((None, nq, d), lambda b, g, pt: (b, 0, 1)),
                pl.BlockSpec((None, nq, d), lambda b, g, pt: (b, 0, 2)),
                pl.BlockSpec((None, nq, LANES), lambda b, g, pt: (b, 0, fblk)),
                pl.BlockSpec((None, nh, nq), lambda b, g, pt: (b, 0, 0)),
                pl.BlockSpec((1, nh), cmap), pl.BlockSpec((nh, 1), cmap),
                pl.BlockSpec((page, page), cmap), pl.BlockSpec((nh, d), cmap)]
    in_specs += [pl.BlockSpec((None, None, d, page), pmap(pi)) for pi in range(pg)]
    in_specs += [pl.BlockSpec((None, None, d, page), pmap(pi)) for pi in range(pg)]
    in_specs += [pl.BlockSpec((None, None, nh, page), pmap(pi)) for pi in range(pg)]
    grid_spec = pltpu.PrefetchScalarGridSpec(
        num_scalar_prefetch=1,
        grid=(nseq, npages // pg),
        in_specs=in_specs,
        out_specs=[pl.BlockSpec((None, nq, d), lambda b, g, pt: (b, 0, 0)),
                   pl.BlockSpec((None, nq, nh), lambda b, g, pt: (b, 0, 0))],
        scratch_shapes=[pltpu.VMEM((rows, d), BF16), pltpu.VMEM((rows, 1), F32), pltpu.VMEM((rows, 1), F32),
                        pltpu.VMEM((rows, d), F32), pltpu.VMEM((nh, 1), F32),
                        pltpu.VMEM((LANES, d), F32), pltpu.VMEM((LANES, d), F32)])
    kern = functools.partial(_attn_sample_kernel, pg=pg, nq=nq, nh=nh, scale=scale)
    return pl.pallas_call(
        kern,
        grid_spec=grid_spec,
        out_shape=[jax.ShapeDtypeStruct((nseq, nq, d), F32), jax.ShapeDtypeStruct((nseq, nq, nh), F32)],
        compiler_params=_cparams(("parallel", "arbitrary")),
        name="fox_attn_sample",
    )(page_table, proj3, proj3, proj3, proj3, ft3, bf_row, bf_col, utri, hmask,
      *([kt_cache] * pg), *([vt_cache] * pg), *([lf_cache] * pg))


def _tri_upper(n):
    r = lax.broadcasted_iota(jnp.int32, (n, n), 0)
    c = lax.broadcasted_iota(jnp.int32, (n, n), 1)
    return (r <= c).astype(BF16)


def kernel(x_prompt, x_sample, cache_k, cache_v, cache_logf, page_table, state_ssm, state_conv,
           norm_mix_w, norm_mlp_w, norm_out_w, fox_w_in, fox_b_f, fox_w_out,
           ssd_w_in, ssd_conv_w, ssd_conv_b, ssd_dt_bias, ssd_a_log, ssd_d, ssd_norm_w, ssd_w_out,
           mlp_w_up, mlp_w_down):
    bp, lp, d = x_prompt.shape
    bs, ls, _ = x_sample.shape
    nh = fox_b_f.shape[-1]
    nsh = ssd_dt_bias.shape[-1]
    di = nsh * HEAD_DIM
    nzx = ssd_w_in.shape[-1] - nsh
    page = cache_k.shape[2]

    fox_wt = jnp.transpose(fox_w_in[0]).astype(BF16)
    wq, wk, wv, wf = fox_wt[0:d], fox_wt[d:2 * d], fox_wt[2 * d:3 * d], fox_wt[3 * d:]
    npad = (-fox_wt.shape[0]) % LANES
    fox_wt_pad = jnp.concatenate([fox_wt, jnp.zeros((npad, d), BF16)], axis=0)
    bf_row = fox_b_f[0].reshape(1, nh)
    bf_col = fox_b_f[0].reshape(nh, 1)
    fox_wo = fox_w_out[0].astype(BF16)
    ssd_wt = jnp.transpose(ssd_w_in[0]).astype(BF16)
    wzx, wdt = ssd_wt[0:nzx], ssd_wt[nzx:]
    wdtp = jnp.concatenate([wdt, jnp.zeros((LANES - nsh, d), BF16)], axis=0)
    dtb_row = jnp.concatenate([ssd_dt_bias[0], jnp.zeros((LANES - nsh,), F32)]).reshape(1, LANES)
    dtb_col = ssd_dt_bias[0].reshape(nsh, 1)
    a_neg = -jnp.exp(ssd_a_log[0])
    a_row = jnp.concatenate([a_neg, jnp.zeros((LANES - nsh,), F32)]).reshape(1, LANES)
    a_col = a_neg.reshape(nsh, 1)
    d_row = jnp.repeat(ssd_d[0], HEAD_DIM).reshape(1, di)
    ssd_nw = ssd_norm_w[0].reshape(1, di)
    ssd_wo = ssd_w_out[0].astype(BF16)
    conv_w = ssd_conv_w[0]
    conv_b = ssd_conv_b[0].reshape(1, -1)
    wup = mlp_w_up.astype(BF16)
    wdn = mlp_w_down.astype(BF16)
    nmix = norm_mix_w.reshape(-1, 1, d)
    nmlp = norm_mlp_w.reshape(-1, 1, d)
    nout = norm_out_w.reshape(1, d)
    utri = _tri_upper(LANES)
    ltri = jnp.transpose(utri)
    utri_c = _tri_upper(256)
    hmask = (lax.broadcasted_iota(jnp.int32, (nh, d), 1) // HEAD_DIM
             == lax.broadcasted_iota(jnp.int32, (nh, d), 0)).astype(F32)

    def ssd_layer(x, batch, seq, h0, tail):
        zx, dtn, dtt = _ssd_in(x, nmix[1], wzx, wdtp, wdt, dtb_row, dtb_col)
        zx3 = zx.reshape(batch, seq, nzx)
        dtn3 = dtn.reshape(batch, seq, LANES)
        dtt3 = jnp.transpose(dtt.reshape(nsh, batch, seq), (1, 0, 2))
        gy, hlast = _ssd_scan(zx3, dtn3, dtt3, h0, tail, conv_w, conv_b, a_row, a_col, d_row, ssd_nw,
                              ltri, utri)
        x = _linear_residual(gy.reshape(batch * seq, di), ssd_wo, x)
        y = _mlp_block(x, nmlp[1], wup[1], wdn[1], nout, final_norm=True)
        new_conv = zx3[:, seq - 3:, di:].reshape(batch, 1, 3, nzx - di)
        return y, hlast.reshape(batch, 1, nsh, HEAD_DIM, SSD_STATE), new_conv

    tp = bp * lp
    xp = x_prompt.reshape(tp, d)
    q, kt, vt, ktb, vtb, lft = _fox_in_prompt(xp, nmix[0], wq, wk, wv, wf, bf_col, batch=bp, seq=lp)
    ct = _cumsum_lanes(lft, utri_c)
    o = _attn_prompt(q, ktb, vtb, ct.reshape(bp, nh // 2, 2, lp), batch=bp, seq=lp)
    xp = _linear_residual(o, fox_wo, xp)
    xp = _mlp_block(xp, nmlp[0], wup[0], wdn[0], nout, final_norm=False)
    h0_p = jnp.zeros((bp, nsh * HEAD_DIM, SSD_STATE), state_ssm.dtype)
    tail_p = jnp.zeros((bp, 8, nzx - di), F32)
    y_p, ssm_p, conv_p = ssd_layer(xp, bp, lp, h0_p, tail_p)
    y_prompt = y_p.reshape(bp, lp, d)
    new_k_prompt = jnp.transpose(kt.reshape(bp, 1, nh, HEAD_DIM, lp), (0, 1, 4, 2, 3))
    new_v_prompt = jnp.transpose(vt.reshape(bp, 1, nh, HEAD_DIM, lp), (0, 1, 4, 2, 3))
    new_logf_prompt = jnp.transpose(lft.reshape(bp, 1, nh, lp), (0, 1, 3, 2))

    ts = bs * ls
    xs = x_sample.reshape(ts, d)
    proj = _norm_linear(xs, nmix[0], fox_wt_pad)
    proj3 = proj.reshape(bs, ls, -1)
    ft3 = jnp.transpose(proj3[:, :, 3 * d:3 * d + nh], (0, 2, 1))
    kt_cache = jnp.transpose(cache_k, (0, 1, 3, 4, 2)).reshape(cache_k.shape[0], cache_k.shape[1], d, page)
    vt_cache = jnp.transpose(cache_v, (0, 1, 3, 4, 2)).reshape(cache_v.shape[0], cache_v.shape[1], d, page)
    lf_cache = jnp.transpose(cache_logf, (0, 1, 3, 2))
    o_s, lf_s = _attn_sample(page_table, proj3, ft3, bf_row, bf_col, utri, hmask,
                             kt_cache, vt_cache, lf_cache, d=d)
    xs = _linear_residual(o_s.reshape(ts, d), fox_wo, xs)
    xs = _mlp_block(xs, nmlp[0], wup[0], wdn[0], nout, final_norm=False)
    h0_s = state_ssm[:, 0].reshape(bs, nsh * HEAD_DIM, SSD_STATE)
    tail_s = jnp.concatenate([jnp.zeros((bs, 5, nzx - di), F32), state_conv[:, 0]], axis=1)
    y_s, ssm_s, conv_s = ssd_layer(xs, bs, ls, h0_s, tail_s)
    y_sample = y_s.reshape(bs, ls, d)
    new_k_sample = proj3[:, :, d:2 * d].reshape(bs, 1, ls, nh, HEAD_DIM)
    new_v_sample = proj3[:, :, 2 * d:3 * d].reshape(bs, 1, ls, nh, HEAD_DIM)
    new_logf_sample = lf_s.reshape(bs, 1, ls, nh)

    return (y_prompt, y_sample, new_k_prompt, new_v_prompt, new_logf_prompt, ssm_p, conv_p,
            new_k_sample, new_v_sample, new_logf_sample, ssm_s, conv_s)
```

```python
import functools

import jax
import jax.numpy as jnp
from jax import lax
from jax.experimental import pallas as pl
from jax.experimental.pallas import tpu as pltpu

F32 = jnp.float32
BF16 = jnp.bfloat16
EPS = 1e-5
LANES = 128
HEAD_DIM = 64
SSD_STATE = 128
SSD_GROUP_W = 256
VMEM_LIMIT = 52 * 1024 * 1024

NT = (((1,), (1,)), ((), ()))
NN = (((1,), (0,)), ((), ()))
TN = (((0,), (0,)), ((), ()))


def _cparams(sem):
    return pltpu.CompilerParams(dimension_semantics=sem, vmem_limit_bytes=VMEM_LIMIT)


def _dot(a, b, dims=NN):
    return lax.dot_general(a, b, dims, preferred_element_type=F32)


def _split3(x):
    hi = x.astype(BF16)
    r = x - hi.astype(F32)
    mid = r.astype(BF16)
    lo = (r - mid.astype(F32)).astype(BF16)
    return hi, mid, lo


def _dot_f32_lhs(x, sel, dims=NN):
    hi, mid, lo = _split3(x)
    return _dot(hi, sel, dims) + _dot(mid, sel, dims) + _dot(lo, sel, dims)


def _dot_f32_rhs(sel, x, dims=NN):
    hi, mid, lo = _split3(x)
    return _dot(sel, hi, dims) + _dot(sel, mid, dims) + _dot(sel, lo, dims)


def _rmsnorm(x, w):
    ms = jnp.mean(x * x, axis=-1, keepdims=True)
    return x * lax.rsqrt(ms + EPS) * w


def _softplus(x):
    return jnp.maximum(x, 0.0) + jnp.log1p(jnp.exp(-jnp.abs(x)))


def _log_sigmoid(x):
    return -_softplus(-x)


def _silu(x):
    hx = 0.5 * x
    return hx + hx * jnp.tanh(hx)


def _fox_in_prompt_kernel(x_ref, nw_ref, wq_ref, wk_ref, wv_ref, wf_ref, bf_ref,
                          q_ref, kt_ref, vt_ref, ktb_ref, vtb_ref, lft_ref, *, scale, nchunk):
    xn = _rmsnorm(x_ref[...], nw_ref[...]).astype(BF16)
    d = xn.shape[1]
    cw = d // nchunk
    for c in range(nchunk):
        sl = slice(c * cw, (c + 1) * cw)
        q = _dot(xn, wq_ref[sl, :], NT)
        q_ref[:, sl] = (q * scale).astype(BF16)
        kt = _dot(wk_ref[sl, :], xn, NT)
        kt_ref[0, sl, :] = kt
        ktb_ref[0, sl, :] = kt.astype(BF16)
        vt = _dot(wv_ref[sl, :], xn, NT)
        vt_ref[0, sl, :] = vt
        vtb_ref[0, sl, :] = vt.astype(BF16)
    ft = _dot(wf_ref[...], xn, NT)
    lft_ref[0] = _log_sigmoid(ft + bf_ref[...])


def _fox_in_prompt(x, nw, wq, wk, wv, wf, bf_col, *, batch, seq, tm=512):
    t, d = x.shape
    nh = wf.shape[0]
    nt = seq // tm
    scale = HEAD_DIM ** -0.5
    row = lambda i: (i, 0)
    full = lambda i: (0, 0)
    tr = lambda i: (i // nt, 0, i % nt)
    kern = functools.partial(_fox_in_prompt_kernel, scale=scale, nchunk=4)
    return pl.pallas_call(
        kern,
        grid=(t // tm,),
        in_specs=[pl.BlockSpec((tm, d), row), pl.BlockSpec((1, d), full),
                  pl.BlockSpec((d, d), full), pl.BlockSpec((d, d), full), pl.BlockSpec((d, d), full),
                  pl.BlockSpec((nh, d), full), pl.BlockSpec((nh, 1), full)],
        out_specs=[pl.BlockSpec((tm, d), row),
                   pl.BlockSpec((1, d, tm), tr), pl.BlockSpec((1, d, tm), tr),
                   pl.BlockSpec((1, d, tm), tr), pl.BlockSpec((1, d, tm), tr),
                   pl.BlockSpec((1, nh, tm), tr)],
        out_shape=[jax.ShapeDtypeStruct((t, d), BF16),
                   jax.ShapeDtypeStruct((batch, d, seq), F32), jax.ShapeDtypeStruct((batch, d, seq), F32),
                   jax.ShapeDtypeStruct((batch, d, seq), BF16), jax.ShapeDtypeStruct((batch, d, seq), BF16),
                   jax.ShapeDtypeStruct((batch, nh, seq), F32)],
        compiler_params=_cparams(("parallel",)),
        name="fox_in_prompt",
    )(x, nw, wq, wk, wv, wf, bf_col)


def _cumsum_kernel(lf_ref, u_ref, c_ref, *, chunk):
    nh, seq = lf_ref.shape[1], lf_ref.shape[2]
    carry = jnp.zeros((nh, 1), F32)
    for j in range(seq // chunk):
        sl = slice(j * chunk, (j + 1) * chunk)
        c = _dot_f32_lhs(lf_ref[0, :, sl], u_ref[...]) + carry
        c_ref[0, :, sl] = c
        carry = c[:, chunk - 1:chunk]


def _cumsum_lanes(lft, utri):
    b, nh, seq = lft.shape
    chunk = utri.shape[0]
    return pl.pallas_call(
        functools.partial(_cumsum_kernel, chunk=chunk),
        grid=(b,),
        in_specs=[pl.BlockSpec((1, nh, seq), lambda i: (i, 0, 0)),
                  pl.BlockSpec((chunk, chunk), lambda i: (0, 0))],
        out_specs=pl.BlockSpec((1, nh, seq), lambda i: (i, 0, 0)),
        out_shape=jax.ShapeDtypeStruct((b, nh, seq), F32),
        compiler_params=_cparams(("parallel",)),
        name="fox_cumsum",
    )(lft, utri)


N_SPLIT = 3


def _attn_prompt_kernel(q_ref, kt_ref, vt_ref, c_ref, o_ref, m_sc, acc_sc, *, blk, rb):
    i = pl.program_id(2)
    q2 = q_ref[...].astype(F32)
    lane = lax.broadcasted_iota(jnp.int32, (blk, LANES), 1)
    lo = lane < HEAD_DIM
    qe = [jnp.where(lo, q2, jnp.where(lane < HEAD_DIM + N_SPLIT, 1.0, 0.0)).astype(BF16),
          jnp.where(lo, jnp.where(lane < N_SPLIT, 1.0, 0.0), q2).astype(BF16)]
    m_sc[...] = jnp.full(m_sc.shape, -jnp.inf, F32)
    acc_sc[...] = jnp.zeros(acc_sc.shape, F32)
    rowi = lax.broadcasted_iota(jnp.int32, (16, blk), 0)
    pad = jnp.zeros((HEAD_DIM - 16, blk), BF16)
    ones_blk = jnp.concatenate([jnp.where(rowi == 0, 1.0, 0.0).astype(BF16), pad], axis=0)
    nrep = blk // LANES

    def step(j, masked):
        off = pl.multiple_of(j * blk, blk)
        ck = c_ref[0, 0, :, pl.ds(off, blk)]
        if masked:
            r = lax.broadcasted_iota(jnp.int32, (blk, blk), 0)
            cidx = lax.broadcasted_iota(jnp.int32, (blk, blk), 1)
            vis = cidx <= r
        for h in range(2):
            hs = slice(h * HEAD_DIM, (h + 1) * HEAD_DIM)
            hi, mid, low = _split3(-ck[h:h + 1, :])
            b16 = jnp.where(rowi == 0, hi.astype(F32),
                            jnp.where(rowi == 1, mid.astype(F32),
                                      jnp.where(rowi == 2, low.astype(F32), 0.0))).astype(BF16)
            bias_blk = jnp.concatenate([b16, pad], axis=0)
            kth = kt_ref[0, hs, pl.ds(off, blk)]
            vth = vt_ref[0, hs, pl.ds(off, blk)]
            if h == 0:
                kte = jnp.concatenate([kth, bias_blk], axis=0)
                vte = jnp.concatenate([vth, ones_blk], axis=0)
            else:
                kte = jnp.concatenate([bias_blk, kth], axis=0)
                vte = jnp.concatenate([ones_blk, vth], axis=0)
            s = _dot(qe[h], kte)
            if masked:
                s = jnp.where(vis, s, -jnp.inf)
            ps, alphas = [], []
            for r0 in range(0, blk, rb):
                sr = s[r0:r0 + rb]
                m_prev = m_sc[h, r0:r0 + rb, :]
                m_new = jnp.maximum(m_prev, jnp.max(sr, axis=-1, keepdims=True))
                alphas.append(jnp.exp(m_prev - m_new))
                ps.append(jnp.exp(sr - jnp.concatenate([m_new] * nrep, axis=1)).astype(BF16))
                m_sc[h, r0:r0 + rb, :] = m_new
            pv = _dot(jnp.concatenate(ps, axis=0), vte, NT)
            acc_sc[h] = jnp.concatenate(alphas, axis=0) * acc_sc[h] + pv

    def body(j, carry):
        step(j, False)
        return carry

    lax.fori_loop(0, i, body, 0)
    step(i, True)
    a0 = acc_sc[0]
    a1 = acc_sc[1]
    out0 = a0 / a0[:, HEAD_DIM:HEAD_DIM + 1]
    out1 = a1 / a1[:, 0:1]
    o_ref[...] = jnp.where(lo, out0, out1).astype(o_ref.dtype)


def _attn_prompt(q, ktb, vtb, c4, *, batch, seq, blk=512, rb=128):
    t, d = q.shape
    npair = d // LANES
    nq = seq // blk
    qmap = lambda b, h, i: (b * nq + i, h)
    kmap = lambda b, h, i: (b, h, 0)
    return pl.pallas_call(
        functools.partial(_attn_prompt_kernel, blk=blk, rb=rb),
        grid=(batch, npair, nq),
        in_specs=[pl.BlockSpec((blk, LANES), qmap),
                  pl.BlockSpec((1, LANES, seq), kmap), pl.BlockSpec((1, LANES, seq), kmap),
                  pl.BlockSpec((1, 1, 2, seq), lambda b, h, i: (b, h, 0, 0))],
        out_specs=pl.BlockSpec((blk, LANES), qmap),
        out_shape=jax.ShapeDtypeStruct((t, d), BF16),
        scratch_shapes=[pltpu.VMEM((2, blk, LANES), F32), pltpu.VMEM((2, blk, LANES), F32)],
        compiler_params=_cparams(("parallel", "parallel", "arbitrary")),
        name="fox_attn_prompt",
    )(q, ktb, vtb, c4)


def _linear_residual_kernel(a_ref, w_ref, x_ref, o_ref):
    o_ref[...] = x_ref[...] + _dot(a_ref[...].astype(BF16), w_ref[...])


def _linear_residual(a, w, x, *, tm=512):
    t, k = a.shape
    n = w.shape[1]
    tm = min(tm, t)
    return pl.pallas_call(
        _linear_residual_kernel,
        grid=(t // tm,),
        in_specs=[pl.BlockSpec((tm, k), lambda i: (i, 0)), pl.BlockSpec((k, n), lambda i: (0, 0)),
                  pl.BlockSpec((tm, n), lambda i: (i, 0))],
        out_specs=pl.BlockSpec((tm, n), lambda i: (i, 0)),
        out_shape=jax.ShapeDtypeStruct((t, n), F32),
        compiler_params=_cparams(("parallel",)),
        name="linear_residual",
    )(a, w, x)


def _mlp_kernel(x_ref, nw_ref, wup_ref, wdn_ref, fw_ref, o_ref, xn_sc, acc_sc, *, final_norm):
    f = pl.program_id(1)

    @pl.when(f == 0)
    def _():
        x = x_ref[...]
        xn_sc[...] = _rmsnorm(x, nw_ref[...]).astype(BF16)
        acc_sc[...] = x

    h = jnp.maximum(_dot(xn_sc[...], wup_ref[...]), 0.0)
    acc_sc[...] += _dot((h * h).astype(BF16), wdn_ref[...])

    @pl.when(f == pl.num_programs(1) - 1)
    def _():
        y = acc_sc[...]
        if final_norm:
            y = _rmsnorm(y, fw_ref[...])
        o_ref[...] = y


def _mlp_block(x, nw, wup, wdn, fw, *, final_norm, tm=1024, tf=1024):
    t, d = x.shape
    dff = wup.shape[1]
    tm = min(tm, t)
    return pl.pallas_call(
        functools.partial(_mlp_kernel, final_norm=final_norm),
        grid=(t // tm, dff // tf),
        in_specs=[pl.BlockSpec((tm, d), lambda i, f: (i, 0)), pl.BlockSpec((1, d), lambda i, f: (0, 0)),
                  pl.BlockSpec((d, tf), lambda i, f: (0, f)), pl.BlockSpec((tf, d), lambda i, f: (f, 0)),
                  pl.BlockSpec((1, d), lambda i, f: (0, 0))],
        out_specs=pl.BlockSpec((tm, d), lambda i, f: (i, 0)),
        out_shape=jax.ShapeDtypeStruct((t, d), F32),
        scratch_shapes=[pltpu.VMEM((tm, d), BF16), pltpu.VMEM((tm, d), F32)],
        compiler_params=_cparams(("parallel", "arbitrary")),
        name="mlp_block",
    )(x, nw, wup, wdn, fw)


def _norm_linear_kernel(x_ref, nw_ref, wt_ref, o_ref):
    xn = _rmsnorm(x_ref[...], nw_ref[...]).astype(BF16)
    o_ref[...] = _dot(xn, wt_ref[...], NT)


def _norm_linear(x, nw, wt, *, tm=128):
    t, d = x.shape
    n = wt.shape[0]
    return pl.pallas_call(
        _norm_linear_kernel,
        grid=(t // tm,),
        in_specs=[pl.BlockSpec((tm, d), lambda i: (i, 0)), pl.BlockSpec((1, d), lambda i: (0, 0)),
                  pl.BlockSpec((n, d), lambda i: (0, 0))],
        out_specs=pl.BlockSpec((tm, n), lambda i: (i, 0)),
        out_shape=jax.ShapeDtypeStruct((t, n), F32),
        compiler_params=_cparams(("parallel",)),
        name="norm_linear",
    )(x, nw, wt)


def _ssd_in_kernel(x_ref, nw_ref, w_ref, wdtp_ref, wdt_ref, brow_ref, bcol_ref,
                   zx_ref, dtn_ref, dtt_ref, xn_sc):
    j = pl.program_id(1)

    @pl.when(j == 0)
    def _():
        xn = _rmsnorm(x_ref[...], nw_ref[...]).astype(BF16)
        xn_sc[...] = xn
        dtn_ref[...] = _softplus(_dot(xn, wdtp_ref[...], NT) + brow_ref[...])
        dtt_ref[...] = _softplus(_dot(wdt_ref[...], xn, NT) + bcol_ref[...])

    zx_ref[...] = _dot(xn_sc[...], w_ref[...], NT)


def _ssd_in(x, nw, wzx, wdtp, wdt, brow, bcol, *, tm=1024, tn=1024):
    t, d = x.shape
    n = wzx.shape[0]
    nh = wdt.shape[0]
    tm = min(tm, t)
    return pl.pallas_call(
        _ssd_in_kernel,
        grid=(t // tm, n // tn),
        in_specs=[pl.BlockSpec((tm, d), lambda i, j: (i, 0)), pl.BlockSpec((1, d), lambda i, j: (0, 0)),
                  pl.BlockSpec((tn, d), lambda i, j: (j, 0)),
                  pl.BlockSpec((LANES, d), lambda i, j: (0, 0)), pl.BlockSpec((nh, d), lambda i, j: (0, 0)),
                  pl.BlockSpec((1, LANES), lambda i, j: (0, 0)), pl.BlockSpec((nh, 1), lambda i, j: (0, 0))],
        out_specs=[pl.BlockSpec((tm, tn), lambda i, j: (i, j)),
                   pl.BlockSpec((tm, LANES), lambda i, j: (i, 0)),
                   pl.BlockSpec((nh, tm), lambda i, j: (0, i))],
        out_shape=[jax.ShapeDtypeStruct((t, n), F32), jax.ShapeDtypeStruct((t, LANES), F32),
                   jax.ShapeDtypeStruct((nh, t), F32)],
        scratch_shapes=[pltpu.VMEM((tm, d), BF16)],
        compiler_params=_cparams(("parallel", "arbitrary")),
        name="ssd_in",
    )(x, nw, wzx, wdtp, wdt, brow, bcol)


def _ssd_scan_kernel(z_ref, xs_ref, bc_ref, dtn_ref, dtt_ref, h0_ref, tx_ref, tbc_ref,
                     cwx_ref, cwbc_ref, cbx_ref, cbbc_ref, arow_ref, acol_ref, drow_ref, nw_ref,
                     ltri_ref, utri_ref,
                     gy_ref, hout_ref,
                     h_sc, xpad_sc, bcpad_sc, dtn_sc, dtt_sc, z_sc, *, q, valid, nheads):
    c = pl.program_id(1)
    ngroups = xs_ref.shape[-1] // SSD_GROUP_W

    @pl.when(c == 0)
    def _():
        h_sc[...] = h0_ref[...]
        xpad_sc[0:8, :] = tx_ref[...]
        bcpad_sc[0:8, :] = tbc_ref[...]

    if valid == q:
        xpad_sc[8:8 + q, :] = xs_ref[...]
        bcpad_sc[8:8 + q, :] = bc_ref[...]
        dtn = dtn_ref[...]
        dtt = dtt_ref[...]
        z_src = z_ref
    else:
        xpad_sc[8:8 + q, :] = jnp.zeros((q, xpad_sc.shape[1]), F32)
        bcpad_sc[8:8 + q, :] = jnp.zeros((q, bcpad_sc.shape[1]), F32)
        xpad_sc[8:8 + valid, :] = xs_ref[...]
        bcpad_sc[8:8 + valid, :] = bc_ref[...]
        dtn_sc[...] = jnp.zeros(dtn_sc.shape, F32)
        dtn_sc[0:valid, :] = dtn_ref[...]
        dtt_sc[...] = jnp.zeros(dtt_sc.shape, F32)
        dtt_sc[:, 0:valid] = dtt_ref[...]
        z_sc[...] = jnp.zeros(z_sc.shape, F32)
        z_sc[0:valid, :] = z_ref[...]
        dtn = dtn_sc[...]
        dtt = dtt_sc[...]
        z_src = z_sc

    def conv(pad_sc, w_ref, b_ref, sl):
        w = w_ref[:, sl]
        acc = b_ref[:, sl] + pad_sc[5:5 + q, sl] * w[0:1, :]
        acc = acc + pad_sc[6:6 + q, sl] * w[1:2, :]
        acc = acc + pad_sc[7:7 + q, sl] * w[2:3, :]
        acc = acc + pad_sc[8:8 + q, sl] * w[3:4, :]
        return _silu(acc)

    a_nat = dtn * arow_ref[...]
    a_t = dtt * acol_ref[...]
    cum_nat = _dot_f32_rhs(ltri_ref[...], a_nat)
    cum_t = _dot_f32_lhs(a_t, utri_ref[...])
    ri = lax.broadcasted_iota(jnp.int32, (q, q), 0)
    ci = lax.broadcasted_iota(jnp.int32, (q, q), 1)
    tri = ci <= ri
    lo = lax.broadcasted_iota(jnp.int32, (q, LANES), 1) < HEAD_DIM
    rlo = lax.broadcasted_iota(jnp.int32, (LANES, SSD_STATE), 0) < HEAD_DIM
    nb = bc_ref.shape[-1] // 2

    for g in range(ngroups):
        bg = conv(bcpad_sc, cwbc_ref, cbbc_ref, slice(g * SSD_STATE, (g + 1) * SSD_STATE))
        cg = conv(bcpad_sc, cwbc_ref, cbbc_ref, slice(nb + g * SSD_STATE, nb + (g + 1) * SSD_STATE))
        bb = bg.astype(BF16)
        cb16 = cg.astype(BF16)
        cb = _dot(cb16, bb, NT)
        gated = []
        ssq = jnp.zeros((q, 1), F32)
        for pr in range(2):
            p = 2 * g + pr
            sl = slice(p * LANES, (p + 1) * LANES)
            xp = conv(xpad_sc, cwx_ref, cbx_ref, sl)
            xpb = xp.astype(BF16)
            yi, ee, we, dl = [], [], [], []
            for r in range(2):
                h = 2 * p + r
                colb = jnp.broadcast_to(cum_nat[:, h:h + 1], (q, LANES))
                rowb = cum_t[h:h + 1, :]
                dec = jnp.exp(jnp.where(tri, colb - rowb, -jnp.inf))
                w = (cb * dec * dtt[h:h + 1, :]).astype(BF16)
                yi.append(_dot(w, xpb))
                ee.append(jnp.exp(colb))
                clast = cum_nat[q - 1:q, h:h + 1]
                we.append(jnp.exp(clast - colb) * jnp.broadcast_to(dtn[:, h:h + 1], (q, LANES)))
                dl.append(jnp.broadcast_to(jnp.exp(clast), (LANES, SSD_STATE)))
            hp = h_sc[sl, :]
            y_inter = _dot(cb16, hp.astype(BF16), NT) * jnp.where(lo, ee[0], ee[1])
            xw = (xp * jnp.where(lo, we[0], we[1])).astype(BF16)
            h_sc[sl, :] = jnp.where(rlo, dl[0], dl[1]) * hp + _dot(xw, bb, TN)
            y = jnp.where(lo, yi[0], yi[1]) + y_inter + xp * drow_ref[:, sl]
            gt = y * _silu(z_src[:, sl])
            ssq = ssq + jnp.sum(gt * gt, axis=-1, keepdims=True)
            gated.append(gt)
        rs = lax.rsqrt(ssq / SSD_GROUP_W + EPS)
        for pr in range(2):
            sl = slice((2 * g + pr) * LANES, (2 * g + pr + 1) * LANES)
            gy_ref[:, sl] = (gated[pr] * rs * nw_ref[:, sl])[0:valid].astype(gy_ref.dtype)

    xpad_sc[0:8, :] = xpad_sc[q:q + 8, :]
    bcpad_sc[0:8, :] = bcpad_sc[q:q + 8, :]

    @pl.when(c == pl.num_programs(1) - 1)
    def _():
        hout_ref[...] = h_sc[...]


def _ssd_scan(zx3, dtn3, dtt3, h0, tail, cw, cb, arow, acol, drow, nw, ltri, utri, *, q=128):
    b, seq, _ = zx3.shape
    nheads = dtt3.shape[1]
    di = nheads * HEAD_DIM
    valid = min(q, seq)
    nc = max(seq // q, 1)
    hrows = nheads * HEAD_DIM
    col = lambda k: (lambda i, c: (i, c, k))
    const2 = lambda k: (lambda i, c: (0, k))
    kern = functools.partial(_ssd_scan_kernel, q=q, valid=valid, nheads=nheads)
    return pl.pallas_call(
        kern,
        grid=(b, nc),
        in_specs=[pl.BlockSpec((None, valid, di), col(0)), pl.BlockSpec((None, valid, di), col(1)),
                  pl.BlockSpec((None, valid, di), col(2)),
                  pl.BlockSpec((None, valid, LANES), lambda i, c: (i, c, 0)),
                  pl.BlockSpec((None, nheads, valid), lambda i, c: (i, 0, c)),
                  pl.BlockSpec((None, hrows, SSD_STATE), lambda i, c: (i, 0, 0)),
                  pl.BlockSpec((None, 8, di), lambda i, c: (i, 0, 0)),
                  pl.BlockSpec((None, 8, di), lambda i, c: (i, 0, 1)),
                  pl.BlockSpec((4, di), const2(0)), pl.BlockSpec((4, di), const2(1)),
                  pl.BlockSpec((1, di), const2(0)), pl.BlockSpec((1, di), const2(1)),
                  pl.BlockSpec((1, LANES), const2(0)), pl.BlockSpec((nheads, 1), const2(0)),
                  pl.BlockSpec((1, di), const2(0)), pl.BlockSpec((1, di), const2(0)),
                  pl.BlockSpec((q, q), const2(0)), pl.BlockSpec((q, q), const2(0))],
        out_specs=[pl.BlockSpec((None, valid, di), lambda i, c: (i, c, 0)),
                   pl.BlockSpec((None, hrows, SSD_STATE), lambda i, c: (i, 0, 0))],
        out_shape=[jax.ShapeDtypeStruct((b, seq, di), BF16 if valid == q else F32),
                   jax.ShapeDtypeStruct((b, hrows, SSD_STATE), F32)],
        scratch_shapes=[pltpu.VMEM((hrows, SSD_STATE), F32),
                        pltpu.VMEM((q + 8, di), F32), pltpu.VMEM((q + 8, di), F32),
                        pltpu.VMEM((q, LANES), F32), pltpu.VMEM((nheads, q), F32),
                        pltpu.VMEM((q, di), F32)],
        compiler_params=_cparams(("parallel", "arbitrary")),
        name="ssd_scan",
    )(zx3, zx3, zx3, dtn3, dtt3, h0, tail, tail, cw, cw, cb, cb, arow, acol, drow, nw, ltri, utri)


def _attn_sample_kernel(pt_ref, q_ref, kn_ref, vn_ref, f_ref, ft_ref, bfr_ref, bfc_ref, u_ref, hm_ref,
                        *refs, pg, nq, nh, scale):
    k_refs, v_refs, lf_refs = refs[0:pg], refs[pg:2 * pg], refs[2 * pg:3 * pg]
    o_ref, lfo_ref = refs[3 * pg], refs[3 * pg + 1]
    qbd_sc, m_sc, l_sc, acc_sc, carry_sc, knp_sc, vnp_sc = refs[3 * pg + 2:]
    g = pl.program_id(1)
    rows = nq * nh
    d = acc_sc.shape[1]

    @pl.when(g == 0)
    def _():
        qv = q_ref[...] * scale
        hm = hm_ref[...]
        for qi in range(nq):
            qbd_sc[qi * nh:(qi + 1) * nh, :] = (qv[qi:qi + 1, :] * hm).astype(BF16)
        m_sc[...] = jnp.full(m_sc.shape, -jnp.inf, F32)
        l_sc[...] = jnp.zeros(l_sc.shape, F32)
        acc_sc[...] = jnp.zeros(acc_sc.shape, F32)
        carry_sc[...] = jnp.zeros(carry_sc.shape, F32)

    def update(s, pv_fn):
        m_prev = m_sc[...]
        m_new = jnp.maximum(m_prev, jnp.max(s, axis=-1, keepdims=True))
        alpha = jnp.exp(m_prev - m_new)
        p = jnp.exp(s - jnp.concatenate([m_new] * (s.shape[1] // LANES), axis=1))
        l_sc[...] = alpha * l_sc[...] + jnp.sum(p, axis=-1, keepdims=True)
        acc_sc[...] = jnp.concatenate([alpha] * (d // LANES), axis=1) * acc_sc[...] + pv_fn(p.astype(BF16))
        m_sc[...] = m_new

    qbd = qbd_sc[...]
    off = carry_sc[...]
    scores = []
    for pi in range(pg):
        loc = _dot_f32_lhs(lf_refs[pi][...], u_ref[...])
        cpage = loc + off
        off = off + loc[:, LANES - 1:LANES]
        kt = k_refs[pi][...].astype(BF16)
        scores.append(_dot(qbd, kt) - jnp.concatenate([cpage] * nq, axis=0))
    carry_sc[...] = off

    def pv_pages(p):
        acc = None
        for pi in range(pg):
            part = _dot(p[:, pi * LANES:(pi + 1) * LANES], v_refs[pi][...].astype(BF16), NT)
            acc = part if acc is None else acc + part
        return acc

    update(jnp.concatenate(scores, axis=1), pv_pages)

    @pl.when(g == pl.num_programs(1) - 1)
    def _():
        lfo_ref[...] = _log_sigmoid(f_ref[:, 0:nh] + bfr_ref[...])
        lft = _log_sigmoid(ft_ref[...] + bfc_ref[...])
        lane = lax.broadcasted_iota(jnp.int32, (nh, LANES), 1)
        cnew = jnp.zeros((nh, LANES), F32)
        run = carry_sc[...]
        for t in range(nq):
            run = run + lft[:, t:t + 1]
            cnew = jnp.where(lane == t, run, cnew)
        knp_sc[...] = jnp.zeros(knp_sc.shape, F32)
        knp_sc[0:nq, :] = kn_ref[...]
        vnp_sc[...] = jnp.zeros(vnp_sc.shape, F32)
        vnp_sc[0:nq, :] = vn_ref[...]
        s = _dot(qbd_sc[...], knp_sc[...].astype(BF16), NT) - jnp.concatenate([cnew] * nq, axis=0)
        key = lax.broadcasted_iota(jnp.int32, (rows, LANES), 1)
        ridx = lax.broadcasted_iota(jnp.int32, (rows, LANES), 0)
        qrow = jnp.zeros((rows, LANES), jnp.int32)
        for qi in range(1, nq):
            qrow = qrow + (ridx >= qi * nh).astype(jnp.int32)
        s = jnp.where(key <= qrow, s, -jnp.inf)
        vnb = vnp_sc[...].astype(BF16)
        update(s, lambda p: _dot(p, vnb))
        out = acc_sc[...] / l_sc[:, 0:1]
        hm = hm_ref[...]
        for qi in range(nq):
            o_ref[qi:qi + 1, :] = jnp.sum(out[qi * nh:(qi + 1) * nh, :] * hm, axis=0,
                                          keepdims=True).astype(o_ref.dtype)


def _attn_sample(page_table, proj3, ft3, bf_row, bf_col, utri, hmask, kt_cache, vt_cache, lf_cache,
                 *, d, pg=8):
    nseq, nq, _ = proj3.shape
    npages = page_table.shape[1]
    nh = hmask.shape[0]
    rows = nq * nh
    scale = HEAD_DIM ** -0.5
    page = kt_cache.shape[-1]
    fblk = (3 * d) // LANES

    def pmap(pi):
        return lambda b, g, pt: (pt[b, g * pg + pi], 0, 0, 0)

    cmap = lambda b, g, pt: (0, 0)
    in_specs = [pl.BlockSpec((None, nq, d), lambda b, g, pt: (b, 0, 0)),
                pl.BlockSpec((None, nq, d), lambda b, g, pt: (b, 0, 1)),
                pl.BlockSpec((None, nq, d), lambda b, g, pt: (b, 0, 2)),
                pl.BlockSpec((None, nq, LANES), lambda b, g, pt: (b, 0, fblk)),
                pl.BlockSpec((None, nh, nq), lambda b, g, pt: (b, 0, 0)),
                pl.BlockSpec((1, nh), cmap), pl.BlockSpec((nh, 1), cmap),
                pl.BlockSpec((page, page), cmap), pl.BlockSpec((nh, d), cmap)]
    in_specs += [pl.BlockSpec((None, None, d, page), pmap(pi)) for pi in range(pg)]
    in_specs += [pl.BlockSpec((None, None, d, page), pmap(pi)) for pi in range(pg)]
    in_specs += [pl.BlockSpec((None, None, nh, page), pmap(pi)) for pi in range(pg)]
    grid_spec = pltpu.PrefetchScalarGridSpec(
        num_scalar_prefetch=1,
        grid=(nseq, npages // pg),
        in_specs=in_specs,
        out_specs=[pl.BlockSpec((None, nq, d), lambda b, g, pt: (b, 0, 0)),
                   pl.BlockSpec((None, nq, nh), lambda b, g, pt: (b, 0, 0))],
        scratch_shapes=[pltpu.VMEM((rows, d), BF16), pltpu.VMEM((rows, LANES), F32),
                        pltpu.VMEM((rows, LANES), F32),
                        pltpu.VMEM((rows, d), F32), pltpu.VMEM((nh, 1), F32),
                        pltpu.VMEM((LANES, d), F32), pltpu.VMEM((LANES, d), F32)])
    kern = functools.partial(_attn_sample_kernel, pg=pg, nq=nq, nh=nh, scale=scale)
    return pl.pallas_call(
        kern,
        grid_spec=grid_spec,
        out_shape=[jax.ShapeDtypeStruct((nseq, nq, d), F32), jax.ShapeDtypeStruct((nseq, nq, nh), F32)],
        compiler_params=_cparams(("parallel", "arbitrary")),
        name="fox_attn_sample",
    )(page_table, proj3, proj3, proj3, proj3, ft3, bf_row, bf_col, utri, hmask,
      *([kt_cache] * pg), *([vt_cache] * pg), *([lf_cache] * pg))


def _tri_upper(n):
    r = lax.broadcasted_iota(jnp.int32, (n, n), 0)
    c = lax.broadcasted_iota(jnp.int32, (n, n), 1)
    return (r <= c).astype(BF16)


def kernel(x_prompt, x_sample, cache_k, cache_v, cache_logf, page_table, state_ssm, state_conv,
           norm_mix_w, norm_mlp_w, norm_out_w, fox_w_in, fox_b_f, fox_w_out,
           ssd_w_in, ssd_conv_w, ssd_conv_b, ssd_dt_bias, ssd_a_log, ssd_d, ssd_norm_w, ssd_w_out,
           mlp_w_up, mlp_w_down):
    bp, lp, d = x_prompt.shape
    bs, ls, _ = x_sample.shape
    nh = fox_b_f.shape[-1]
    nsh = ssd_dt_bias.shape[-1]
    di = nsh * HEAD_DIM
    nzx = ssd_w_in.shape[-1] - nsh
    page = cache_k.shape[2]

    fox_wt = jnp.transpose(fox_w_in[0]).astype(BF16)
    wq, wk, wv, wf = fox_wt[0:d], fox_wt[d:2 * d], fox_wt[2 * d:3 * d], fox_wt[3 * d:]
    npad = (-fox_wt.shape[0]) % LANES
    fox_wt_pad = jnp.concatenate([fox_wt, jnp.zeros((npad, d), BF16)], axis=0)
    bf_row = fox_b_f[0].reshape(1, nh)
    bf_col = fox_b_f[0].reshape(nh, 1)
    fox_wo = fox_w_out[0].astype(BF16)
    ssd_wt = jnp.transpose(ssd_w_in[0]).astype(BF16)
    wzx, wdt = ssd_wt[0:nzx], ssd_wt[nzx:]
    wdtp = jnp.concatenate([wdt, jnp.zeros((LANES - nsh, d), BF16)], axis=0)
    dtb_row = jnp.concatenate([ssd_dt_bias[0], jnp.zeros((LANES - nsh,), F32)]).reshape(1, LANES)
    dtb_col = ssd_dt_bias[0].reshape(nsh, 1)
    a_neg = -jnp.exp(ssd_a_log[0])
    a_row = jnp.concatenate([a_neg, jnp.zeros((LANES - nsh,), F32)]).reshape(1, LANES)
    a_col = a_neg.reshape(nsh, 1)
    d_row = jnp.repeat(ssd_d[0], HEAD_DIM).reshape(1, di)
    ssd_nw = ssd_norm_w[0].reshape(1, di)
    ssd_wo = ssd_w_out[0].astype(BF16)
    conv_w = ssd_conv_w[0]
    conv_b = ssd_conv_b[0].reshape(1, -1)
    wup = mlp_w_up.astype(BF16)
    wdn = mlp_w_down.astype(BF16)
    nmix = norm_mix_w.reshape(-1, 1, d)
    nmlp = norm_mlp_w.reshape(-1, 1, d)
    nout = norm_out_w.reshape(1, d)
    utri = _tri_upper(LANES)
    ltri = jnp.transpose(utri)
    utri_c = _tri_upper(256)
    hmask = (lax.broadcasted_iota(jnp.int32, (nh, d), 1) // HEAD_DIM
             == lax.broadcasted_iota(jnp.int32, (nh, d), 0)).astype(F32)

    def ssd_layer(x, batch, seq, h0, tail):
        zx, dtn, dtt = _ssd_in(x, nmix[1], wzx, wdtp, wdt, dtb_row, dtb_col)
        zx3 = zx.reshape(batch, seq, nzx)
        dtn3 = dtn.reshape(batch, seq, LANES)
        dtt3 = jnp.transpose(dtt.reshape(nsh, batch, seq), (1, 0, 2))
        gy, hlast = _ssd_scan(zx3, dtn3, dtt3, h0, tail, conv_w, conv_b, a_row, a_col, d_row, ssd_nw,
                              ltri, utri)
        x = _linear_residual(gy.reshape(batch * seq, di), ssd_wo, x)
        y = _mlp_block(x, nmlp[1], wup[1], wdn[1], nout, final_norm=True)
        new_conv = zx3[:, seq - 3:, di:].reshape(batch, 1, 3, nzx - di)
        return y, hlast.reshape(batch, 1, nsh, HEAD_DIM, SSD_STATE), new_conv

    tp = bp * lp
    xp = x_prompt.reshape(tp, d)
    q, kt, vt, ktb, vtb, lft = _fox_in_prompt(xp, nmix[0], wq, wk, wv, wf, bf_col, batch=bp, seq=lp)
    ct = _cumsum_lanes(lft, utri_c)
    o = _attn_prompt(q, ktb, vtb, ct.reshape(bp, nh // 2, 2, lp), batch=bp, seq=lp)
    xp = _linear_residual(o, fox_wo, xp)
    xp = _mlp_block(xp, nmlp[0], wup[0], wdn[0], nout, final_norm=False)
    h0_p = jnp.zeros((bp, nsh * HEAD_DIM, SSD_STATE), state_ssm.dtype)
    tail_p = jnp.zeros((bp, 8, nzx - di), F32)
    y_p, ssm_p, conv_p = ssd_layer(xp, bp, lp, h0_p, tail_p)
    y_prompt = y_p.reshape(bp, lp, d)
    new_k_prompt = jnp.transpose(kt.reshape(bp, 1, nh, HEAD_DIM, lp), (0, 1, 4, 2, 3))
    new_v_prompt = jnp.transpose(vt.reshape(bp, 1, nh, HEAD_DIM, lp), (0, 1, 4, 2, 3))
    new_logf_prompt = jnp.transpose(lft.reshape(bp, 1, nh, lp), (0, 1, 3, 2))

    ts = bs * ls
    xs = x_sample.reshape(ts, d)
    proj = _norm_linear(xs, nmix[0], fox_wt_pad)
    proj3 = proj.reshape(bs, ls, -1)
    ft3 = jnp.transpose(proj3[:, :, 3 * d:3 * d + nh], (0, 2, 1))
    kt_cache = jnp.transpose(cache_k, (0, 1, 3, 4, 2)).reshape(cache_k.shape[0], cache_k.shape[1], d, page)
    vt_cache = jnp.transpose(cache_v, (0, 1, 3, 4, 2)).reshape(cache_v.shape[0], cache_v.shape[1], d, page)
    lf_cache = jnp.transpose(cache_logf, (0, 1, 3, 2))
    o_s, lf_s = _attn_sample(page_table, proj3, ft3, bf_row, bf_col, utri, hmask,
                             kt_cache, vt_cache, lf_cache, d=d)
    xs = _linear_residual(o_s.reshape(ts, d), fox_wo, xs)
    xs = _mlp_block(xs, nmlp[0], wup[0], wdn[0], nout, final_norm=False)
    h0_s = state_ssm[:, 0].reshape(bs, nsh * HEAD_DIM, SSD_STATE)
    tail_s = jnp.concatenate([jnp.zeros((bs, 5, nzx - di), F32), state_conv[:, 0]], axis=1)
    y_s, ssm_s, conv_s = ssd_layer(xs, bs, ls, h0_s, tail_s)
    y_sample = y_s.reshape(bs, ls, d)
    new_k_sample = proj3[:, :, d:2 * d].reshape(bs, 1, ls, nh, HEAD_DIM)
    new_v_sample = proj3[:, :, 2 * d:3 * d].reshape(bs, 1, ls, nh, HEAD_DIM)
    new_logf_sample = lf_s.reshape(bs, 1, ls, nh)

    return (y_prompt, y_sample, new_k_prompt, new_v_prompt, new_logf_prompt, ssm_p, conv_p,
            new_k_sample, new_v_sample, new_logf_sample, ssm_s, conv_s)
```

```python
import functools

import jax
import jax.numpy as jnp
from jax import lax
from jax.experimental import pallas as pl
from jax.experimental.pallas import tpu as pltpu

F32 = jnp.float32
BF16 = jnp.bfloat16
EPS = 1e-5
LANES = 128
HEAD_DIM = 64
SSD_STATE = 128
SSD_GROUP_W = 256
VMEM_LIMIT = 52 * 1024 * 1024

NT = (((1,), (1,)), ((), ()))
NN = (((1,), (0,)), ((), ()))
TN = (((0,), (0,)), ((), ()))


def _cparams(sem):
    return pltpu.CompilerParams(dimension_semantics=sem, vmem_limit_bytes=VMEM_LIMIT)


def _dot(a, b, dims=NN):
    return lax.dot_general(a, b, dims, preferred_element_type=F32)


def _split3(x):
    hi = x.astype(BF16)
    r = x - hi.astype(F32)
    mid = r.astype(BF16)
    lo = (r - mid.astype(F32)).astype(BF16)
    return hi, mid, lo


def _dot_f32_lhs(x, sel, dims=NN):
    hi, mid, lo = _split3(x)
    return _dot(hi, sel, dims) + _dot(mid, sel, dims) + _dot(lo, sel, dims)


def _dot_f32_rhs(sel, x, dims=NN):
    hi, mid, lo = _split3(x)
    return _dot(sel, hi, dims) + _dot(sel, mid, dims) + _dot(sel, lo, dims)


def _rmsnorm(x, w):
    ms = jnp.mean(x * x, axis=-1, keepdims=True)
    return x * lax.rsqrt(ms + EPS) * w


def _softplus(x):
    return jnp.maximum(x, 0.0) + jnp.log1p(jnp.exp(-jnp.abs(x)))


def _log_sigmoid(x):
    return -_softplus(-x)


def _silu(x):
    hx = 0.5 * x
    return hx + hx * jnp.tanh(hx)


def _fox_in_prompt_kernel(x_ref, nw_ref, wq_ref, wk_ref, wv_ref, wf_ref, bf_ref,
                          q_ref, kt_ref, vt_ref, ktb_ref, vtb_ref, lft_ref, *, scale, nchunk):
    xn = _rmsnorm(x_ref[...], nw_ref[...]).astype(BF16)
    d = xn.shape[1]
    cw = d // nchunk
    for c in range(nchunk):
        sl = slice(c * cw, (c + 1) * cw)
        q = _dot(xn, wq_ref[sl, :], NT)
        q_ref[:, sl] = (q * scale).astype(BF16)
        kt = _dot(wk_ref[sl, :], xn, NT)
        kt_ref[0, sl, :] = kt
        ktb_ref[0, sl, :] = kt.astype(BF16)
        vt = _dot(wv_ref[sl, :], xn, NT)
        vt_ref[0, sl, :] = vt
        vtb_ref[0, sl, :] = vt.astype(BF16)
    ft = _dot(wf_ref[...], xn, NT)
    lft_ref[0] = _log_sigmoid(ft + bf_ref[...])


def _fox_in_prompt(x, nw, wq, wk, wv, wf, bf_col, *, batch, seq, tm=512):
    t, d = x.shape
    nh = wf.shape[0]
    nt = seq // tm
    scale = HEAD_DIM ** -0.5
    row = lambda i: (i, 0)
    full = lambda i: (0, 0)
    tr = lambda i: (i // nt, 0, i % nt)
    kern = functools.partial(_fox_in_prompt_kernel, scale=scale, nchunk=4)
    return pl.pallas_call(
        kern,
        grid=(t // tm,),
        in_specs=[pl.BlockSpec((tm, d), row), pl.BlockSpec((1, d), full),
                  pl.BlockSpec((d, d), full), pl.BlockSpec((d, d), full), pl.BlockSpec((d, d), full),
                  pl.BlockSpec((nh, d), full), pl.BlockSpec((nh, 1), full)],
        out_specs=[pl.BlockSpec((tm, d), row),
                   pl.BlockSpec((1, d, tm), tr), pl.BlockSpec((1, d, tm), tr),
                   pl.BlockSpec((1, d, tm), tr), pl.BlockSpec((1, d, tm), tr),
                   pl.BlockSpec((1, nh, tm), tr)],
        out_shape=[jax.ShapeDtypeStruct((t, d), BF16),
                   jax.ShapeDtypeStruct((batch, d, seq), F32), jax.ShapeDtypeStruct((batch, d, seq), F32),
                   jax.ShapeDtypeStruct((batch, d, seq), BF16), jax.ShapeDtypeStruct((batch, d, seq), BF16),
                   jax.ShapeDtypeStruct((batch, nh, seq), F32)],
        compiler_params=_cparams(("parallel",)),
        name="fox_in_prompt",
    )(x, nw, wq, wk, wv, wf, bf_col)


def _cumsum_kernel(lf_ref, u_ref, c_ref, *, chunk):
    nh, seq = lf_ref.shape[1], lf_ref.shape[2]
    carry = jnp.zeros((nh, 1), F32)
    for j in range(seq // chunk):
        sl = slice(j * chunk, (j + 1) * chunk)
        c = _dot_f32_lhs(lf_ref[0, :, sl], u_ref[...]) + carry
        c_ref[0, :, sl] = c
        carry = c[:, chunk - 1:chunk]


def _cumsum_lanes(lft, utri):
    b, nh, seq = lft.shape
    chunk = utri.shape[0]
    return pl.pallas_call(
        functools.partial(_cumsum_kernel, chunk=chunk),
        grid=(b,),
        in_specs=[pl.BlockSpec((1, nh, seq), lambda i: (i, 0, 0)),
                  pl.BlockSpec((chunk, chunk), lambda i: (0, 0))],
        out_specs=pl.BlockSpec((1, nh, seq), lambda i: (i, 0, 0)),
        out_shape=jax.ShapeDtypeStruct((b, nh, seq), F32),
        compiler_params=_cparams(("parallel",)),
        name="fox_cumsum",
    )(lft, utri)


N_SPLIT = 3


def _attn_prompt_kernel(q_ref, kt_ref, vt_ref, c_ref, o_ref, m_sc, acc_sc, *, blk, wide, dsplit):
    i = pl.program_id(2)
    q2 = q_ref[...].astype(F32)
    lane = lax.broadcasted_iota(jnp.int32, (blk, LANES), 1)
    lo = lane < HEAD_DIM
    qe = [jnp.where(lo, q2, jnp.where(lane < HEAD_DIM + N_SPLIT, 1.0, 0.0)).astype(BF16),
          jnp.where(lo, jnp.where(lane < N_SPLIT, 1.0, 0.0), q2).astype(BF16)]
    m_sc[...] = jnp.full(m_sc.shape, -jnp.inf, F32)
    acc_sc[...] = jnp.zeros(acc_sc.shape, F32)
    hb = blk // dsplit
    consts = {}
    for wk in sorted({blk, wide * blk} | {(t + 1) * hb for t in range(dsplit)}):
        rowi = lax.broadcasted_iota(jnp.int32, (16, wk), 0)
        pad = jnp.zeros((HEAD_DIM - 16, wk), BF16)
        ones_blk = jnp.concatenate([jnp.where(rowi == 0, 1.0, 0.0).astype(BF16), pad], axis=0)
        consts[wk] = (rowi, pad, ones_blk)

    def step(off, wk, r0=0, nr=blk, shift=None):
        rowi, pad, ones_blk = consts[wk]
        nrep = wk // LANES
        rows = slice(r0, r0 + nr)
        ck = c_ref[0, 0, :, pl.ds(off, wk)]
        if shift is not None:
            r = lax.broadcasted_iota(jnp.int32, (nr, wk), 0)
            cidx = lax.broadcasted_iota(jnp.int32, (nr, wk), 1)
            vis = cidx <= r + shift
        scores, vtes = [], []
        for h in range(2):
            hs = slice(h * HEAD_DIM, (h + 1) * HEAD_DIM)
            hi, mid, low = _split3(-ck[h:h + 1, :])
            b16 = jnp.where(rowi == 0, hi.astype(F32),
                            jnp.where(rowi == 1, mid.astype(F32),
                                      jnp.where(rowi == 2, low.astype(F32), 0.0))).astype(BF16)
            bias_blk = jnp.concatenate([b16, pad], axis=0)
            kth = kt_ref[0, hs, pl.ds(off, wk)]
            vth = vt_ref[0, hs, pl.ds(off, wk)]
            if h == 0:
                kte = jnp.concatenate([kth, bias_blk], axis=0)
                vtes.append(jnp.concatenate([vth, ones_blk], axis=0))
            else:
                kte = jnp.concatenate([bias_blk, kth], axis=0)
                vtes.append(jnp.concatenate([ones_blk, vth], axis=0))
            s = _dot(qe[h][rows], kte)
            if shift is not None:
                s = jnp.where(vis, s, -jnp.inf)
            scores.append(s)
        probs, alphas = [], []
        for h in range(2):
            m_prev = m_sc[h, rows, :]
            m_new = jnp.maximum(m_prev, jnp.max(scores[h], axis=-1, keepdims=True))
            alphas.append(jnp.exp(m_prev - m_new))
            probs.append(jnp.exp(scores[h] - jnp.concatenate([m_new] * nrep, axis=1)).astype(BF16))
            m_sc[h, rows, :] = m_new
        for h in range(2):
            pv = _dot(probs[h], vtes[h], NT)
            acc_sc[h, rows, :] = alphas[h] * acc_sc[h, rows, :] + pv

    def body(j, carry):
        step(pl.multiple_of(j * (wide * blk), wide * blk), wide * blk)
        return carry

    nwide = i // wide
    lax.fori_loop(0, nwide, body, 0)
    for extra in range(wide - 1):
        @pl.when(nwide * wide + extra < i)
        def _():
            step(pl.multiple_of((nwide * wide + extra) * blk, blk), blk)
    for t in range(dsplit):
        step(pl.multiple_of(i * blk, blk), (t + 1) * hb, r0=t * hb, nr=hb, shift=t * hb)
    a0 = acc_sc[0]
    a1 = acc_sc[1]
    out0 = a0 / a0[:, HEAD_DIM:HEAD_DIM + 1]
    out1 = a1 / a1[:, 0:1]
    o_ref[...] = jnp.where(lo, out0, out1).astype(o_ref.dtype)


def _attn_prompt(q, ktb, vtb, c4, *, batch, seq, blk=512, wide=2, dsplit=1):
    t, d = q.shape
    npair = d // LANES
    nq = seq // blk
    qmap = lambda b, h, i: (b * nq + i, h)
    kmap = lambda b, h, i: (b, h, 0)
    return pl.pallas_call(
        functools.partial(_attn_prompt_kernel, blk=blk, wide=wide, dsplit=dsplit),
        grid=(batch, npair, nq),
        in_specs=[pl.BlockSpec((blk, LANES), qmap),
                  pl.BlockSpec((1, LANES, seq), kmap), pl.BlockSpec((1, LANES, seq), kmap),
                  pl.BlockSpec((1, 1, 2, seq), lambda b, h, i: (b, h, 0, 0))],
        out_specs=pl.BlockSpec((blk, LANES), qmap),
        out_shape=jax.ShapeDtypeStruct((t, d), BF16),
        scratch_shapes=[pltpu.VMEM((2, blk, LANES), F32), pltpu.VMEM((2, blk, LANES), F32)],
        compiler_params=_cparams(("parallel", "parallel", "arbitrary")),
        name="fox_attn_prompt",
    )(q, ktb, vtb, c4)


def _linear_residual_kernel(a_ref, w_ref, x_ref, o_ref):
    o_ref[...] = x_ref[...] + _dot(a_ref[...].astype(BF16), w_ref[...])


def _linear_residual(a, w, x, *, tm=512):
    t, k = a.shape
    n = w.shape[1]
    tm = min(tm, t)
    return pl.pallas_call(
        _linear_residual_kernel,
        grid=(t // tm,),
        in_specs=[pl.BlockSpec((tm, k), lambda i: (i, 0)), pl.BlockSpec((k, n), lambda i: (0, 0)),
                  pl.BlockSpec((tm, n), lambda i: (i, 0))],
        out_specs=pl.BlockSpec((tm, n), lambda i: (i, 0)),
        out_shape=jax.ShapeDtypeStruct((t, n), F32),
        compiler_params=_cparams(("parallel",)),
        name="linear_residual",
    )(a, w, x)


def _mlp_kernel(x_ref, nw_ref, wup_ref, wdn_ref, fw_ref, o_ref, xn_sc, acc_sc, *, final_norm):
    f = pl.program_id(1)

    @pl.when(f == 0)
    def _():
        x = x_ref[...]
        xn_sc[...] = _rmsnorm(x, nw_ref[...]).astype(BF16)
        acc_sc[...] = x

    h = jnp.maximum(_dot(xn_sc[...], wup_ref[...]), 0.0)
    acc_sc[...] += _dot((h * h).astype(BF16), wdn_ref[...])

    @pl.when(f == pl.num_programs(1) - 1)
    def _():
        y = acc_sc[...]
        if final_norm:
            y = _rmsnorm(y, fw_ref[...])
        o_ref[...] = y


def _mlp_block(x, nw, wup, wdn, fw, *, final_norm, tm=1024, tf=1024):
    t, d = x.shape
    dff = wup.shape[1]
    tm = min(tm, t)
    return pl.pallas_call(
        functools.partial(_mlp_kernel, final_norm=final_norm),
        grid=(t // tm, dff // tf),
        in_specs=[pl.BlockSpec((tm, d), lambda i, f: (i, 0)), pl.BlockSpec((1, d), lambda i, f: (0, 0)),
                  pl.BlockSpec((d, tf), lambda i, f: (0, f)), pl.BlockSpec((tf, d), lambda i, f: (f, 0)),
                  pl.BlockSpec((1, d), lambda i, f: (0, 0))],
        out_specs=pl.BlockSpec((tm, d), lambda i, f: (i, 0)),
        out_shape=jax.ShapeDtypeStruct((t, d), F32),
        scratch_shapes=[pltpu.VMEM((tm, d), BF16), pltpu.VMEM((tm, d), F32)],
        compiler_params=_cparams(("parallel", "arbitrary")),
        name="mlp_block",
    )(x, nw, wup, wdn, fw)


def _norm_linear_kernel(x_ref, nw_ref, wt_ref, o_ref):
    xn = _rmsnorm(x_ref[...], nw_ref[...]).astype(BF16)
    o_ref[...] = _dot(xn, wt_ref[...], NT)


def _norm_linear(x, nw, wt, *, tm=128):
    t, d = x.shape
    n = wt.shape[0]
    return pl.pallas_call(
        _norm_linear_kernel,
        grid=(t // tm,),
        in_specs=[pl.BlockSpec((tm, d), lambda i: (i, 0)), pl.BlockSpec((1, d), lambda i: (0, 0)),
                  pl.BlockSpec((n, d), lambda i: (0, 0))],
        out_specs=pl.BlockSpec((tm, n), lambda i: (i, 0)),
        out_shape=jax.ShapeDtypeStruct((t, n), F32),
        compiler_params=_cparams(("parallel",)),
        name="norm_linear",
    )(x, nw, wt)


def _ssd_mixer_kernel(x_ref, nwx_ref, wzx_ref, wdtp_ref, wdt_ref, brow_ref, bcol_ref,
                      h0_ref, tx_ref, tbc_ref,
                      cwx_ref, cwbc_ref, cbx_ref, cbbc_ref, arow_ref, acol_ref, drow_ref, nw_ref,
                      ltri_ref, utri_ref,
                      gy_ref, hout_ref, txo_ref, tbco_ref,
                      h_sc, xpad_sc, bcpad_sc, z_sc, dtn_sc, dtt_sc, x_sc,
                      sz_sc, sx_sc, sbc_sc, sdtn_sc, sdtt_sc, *, q, valid, nheads, ncol, pipe):
    c = pl.program_id(1)
    di = z_sc.shape[1]
    ngroups = di // SSD_GROUP_W
    first_scan = 1 if pipe else 0

    @pl.when(c == 0)
    def _():
        if pipe:
            for ref in (sz_sc, sx_sc, sbc_sc, sdtn_sc, sdtt_sc):
                ref[...] = jnp.zeros(ref.shape, F32)
        xpad_sc[0:8, :] = tx_ref[...]
        bcpad_sc[0:8, :] = tbc_ref[...]

    @pl.when(c <= first_scan)
    def _():
        h_sc[...] = h0_ref[...]

    if pipe:
        z_sc[...] = sz_sc[...]
        xpad_sc[8:8 + q, :] = sx_sc[...]
        bcpad_sc[8:8 + q, :] = sbc_sc[...]
        dtn_sc[...] = sdtn_sc[...]
        dtt_sc[...] = sdtt_sc[...]
        pz, px, pbc, pdtn, pdtt, prow = sz_sc, sx_sc, sbc_sc, sdtn_sc, sdtt_sc, 0
    else:
        pz, px, pbc, pdtn, pdtt, prow = z_sc, xpad_sc, bcpad_sc, dtn_sc, dtt_sc, 8

    if valid == q:
        x = x_ref[...]
    else:
        x_sc[...] = jnp.zeros(x_sc.shape, F32)
        x_sc[0:valid, :] = x_ref[...]
        x = x_sc[...]
    xn = _rmsnorm(x, nwx_ref[...]).astype(BF16)
    dtn_p = _softplus(_dot(xn, wdtp_ref[...], NT) + brow_ref[...])
    dtt_p = _softplus(_dot(wdt_ref[...], xn, NT) + bcol_ref[...])
    if valid != q:
        dtn_p = jnp.where(lax.broadcasted_iota(jnp.int32, dtn_p.shape, 0) < valid, dtn_p, 0.0)
        dtt_p = jnp.where(lax.broadcasted_iota(jnp.int32, dtt_p.shape, 1) < valid, dtt_p, 0.0)
    pdtn[...] = dtn_p
    pdtt[...] = dtt_p
    cw = (3 * di) // ncol

    def project(k):
        piece = _dot(xn, wzx_ref[k * cw:(k + 1) * cw, :], NT)
        tgt, off = divmod(k * cw, di)
        if tgt == 0:
            pz[:, off:off + cw] = piece
        elif tgt == 1:
            px[prow:prow + q, off:off + cw] = piece
        else:
            pbc[prow:prow + q, off:off + cw] = piece

    if not pipe:
        for k in range(ncol):
            project(k)

    def conv(pad_sc, w_ref, b_ref, sl):
        w = w_ref[:, sl]
        acc = b_ref[:, sl] + pad_sc[5:5 + q, sl] * w[0:1, :]
        acc = acc + pad_sc[6:6 + q, sl] * w[1:2, :]
        acc = acc + pad_sc[7:7 + q, sl] * w[2:3, :]
        acc = acc + pad_sc[8:8 + q, sl] * w[3:4, :]
        return _silu(acc)

    dtn = dtn_sc[...]
    dtt = dtt_sc[...]
    a_nat = dtn * arow_ref[...]
    a_t = dtt * acol_ref[...]
    cum_nat = _dot_f32_rhs(ltri_ref[...], a_nat)
    cum_t = _dot_f32_lhs(a_t, utri_ref[...])
    ri = lax.broadcasted_iota(jnp.int32, (q, q), 0)
    ci = lax.broadcasted_iota(jnp.int32, (q, q), 1)
    tri = ci <= ri
    lo = lax.broadcasted_iota(jnp.int32, (q, LANES), 1) < HEAD_DIM
    rlo = lax.broadcasted_iota(jnp.int32, (LANES, SSD_STATE), 0) < HEAD_DIM
    nb = di // 2
    clast_row = cum_nat[q - 1:q, :]
    e_nat = jnp.exp(cum_nat)
    w_nat = jnp.exp(clast_row - cum_nat) * dtn
    dl_row = jnp.exp(clast_row)

    for g in range(ngroups):
        if pipe:
            for k in range(g * ncol // ngroups, (g + 1) * ncol // ngroups):
                project(k)
        bg = conv(bcpad_sc, cwbc_ref, cbbc_ref, slice(g * SSD_STATE, (g + 1) * SSD_STATE))
        cg = conv(bcpad_sc, cwbc_ref, cbbc_ref, slice(nb + g * SSD_STATE, nb + (g + 1) * SSD_STATE))
        bb = bg.astype(BF16)
        cb16 = cg.astype(BF16)
        cb = _dot(cb16, bb, NT)
        gated = []
        ssq = jnp.zeros((q, 1), F32)
        for pr in range(2):
            p = 2 * g + pr
            sl = slice(p * LANES, (p + 1) * LANES)
            xp = conv(xpad_sc, cwx_ref, cbx_ref, sl)
            xpb = xp.astype(BF16)
            yi, ee, we, dl = [], [], [], []
            for r in range(2):
                h = 2 * p + r
                colb = jnp.broadcast_to(cum_nat[:, h:h + 1], (q, LANES))
                rowb = cum_t[h:h + 1, :]
                dec = jnp.exp(jnp.where(tri, colb - rowb, -jnp.inf))
                w = (cb * dec * dtt[h:h + 1, :]).astype(BF16)
                yi.append(_dot(w, xpb))
                ee.append(jnp.broadcast_to(e_nat[:, h:h + 1], (q, LANES)))
                we.append(jnp.broadcast_to(w_nat[:, h:h + 1], (q, LANES)))
                dl.append(jnp.broadcast_to(dl_row[:, h:h + 1], (LANES, SSD_STATE)))
            hp = h_sc[sl, :]
            y_inter = _dot(cb16, hp.astype(BF16), NT) * jnp.where(lo, ee[0], ee[1])
            xw = (xp * jnp.where(lo, we[0], we[1])).astype(BF16)
            h_sc[sl, :] = jnp.where(rlo, dl[0], dl[1]) * hp + _dot(xw, bb, TN)
            y = jnp.where(lo, yi[0], yi[1]) + y_inter + xp * drow_ref[:, sl]
            gt = y * _silu(z_sc[:, sl])
            ssq = ssq + jnp.sum(gt * gt, axis=-1, keepdims=True)
            gated.append(gt)
        rs = lax.rsqrt(ssq / SSD_GROUP_W + EPS)
        for pr in range(2):
            sl = slice((2 * g + pr) * LANES, (2 * g + pr + 1) * LANES)
            gy_ref[:, sl] = (gated[pr] * rs * nw_ref[:, sl])[0:valid].astype(gy_ref.dtype)

    @pl.when(c == pl.num_programs(1) - 1)
    def _():
        hout_ref[...] = h_sc[...]
        txo_ref[...] = xpad_sc[valid:valid + 8, :]
        tbco_ref[...] = bcpad_sc[valid:valid + 8, :]

    @pl.when(c >= first_scan)
    def _():
        xpad_sc[0:8, :] = xpad_sc[q:q + 8, :]
        bcpad_sc[0:8, :] = bcpad_sc[q:q + 8, :]


def _ssd_mixer(x3, nwx, wzx, wdtp, wdt, brow, bcol, h0, tail, cw, cb, arow, acol, drow, nw, ltri, utri,
               *, q=128, ncol=6):
    b, seq, d = x3.shape
    nheads = wdt.shape[0]
    di = nheads * HEAD_DIM
    valid = min(q, seq)
    nc = max(seq // q, 1)
    pipe = nc > 1
    nsteps = nc + 1 if pipe else nc
    hrows = nheads * HEAD_DIM
    const2 = lambda k: (lambda i, c: (0, k))
    stage = [pltpu.VMEM((q, di), F32)] * 3 + [pltpu.VMEM((q, LANES), F32), pltpu.VMEM((nheads, q), F32)]
    if not pipe:
        stage = [pltpu.VMEM((8, LANES), F32)] * 5
    if pipe:
        ncol = 3 * (di // SSD_GROUP_W)
    kern = functools.partial(_ssd_mixer_kernel, q=q, valid=valid, nheads=nheads, ncol=ncol, pipe=pipe)
    return pl.pallas_call(
        kern,
        grid=(b, nsteps),
        in_specs=[pl.BlockSpec((None, valid, d), lambda i, c: (i, jnp.minimum(c, nc - 1), 0)),
                  pl.BlockSpec((1, d), const2(0)), pl.BlockSpec((3 * di, d), const2(0)),
                  pl.BlockSpec((LANES, d), const2(0)), pl.BlockSpec((nheads, d), const2(0)),
                  pl.BlockSpec((1, LANES), const2(0)), pl.BlockSpec((nheads, 1), const2(0)),
                  pl.BlockSpec((None, hrows, SSD_STATE), lambda i, c: (i, 0, 0)),
                  pl.BlockSpec((None, 8, di), lambda i, c: (i, 0, 0)),
                  pl.BlockSpec((None, 8, di), lambda i, c: (i, 0, 1)),
                  pl.BlockSpec((4, di), const2(0)), pl.BlockSpec((4, di), const2(1)),
                  pl.BlockSpec((1, di), const2(0)), pl.BlockSpec((1, di), const2(1)),
                  pl.BlockSpec((1, LANES), const2(0)), pl.BlockSpec((nheads, 1), const2(0)),
                  pl.BlockSpec((1, di), const2(0)), pl.BlockSpec((1, di), const2(0)),
                  pl.BlockSpec((q, q), const2(0)), pl.BlockSpec((q, q), const2(0))],
        out_specs=[pl.BlockSpec((None, valid, di), lambda i, c: (i, jnp.maximum(c - (nsteps - nc), 0), 0)),
                   pl.BlockSpec((None, hrows, SSD_STATE), lambda i, c: (i, 0, 0)),
                   pl.BlockSpec((None, 8, di), lambda i, c: (i, 0, 0)),
                   pl.BlockSpec((None, 8, di), lambda i, c: (i, 0, 0))],
        out_shape=[jax.ShapeDtypeStruct((b, seq, di), BF16 if valid == q else F32),
                   jax.ShapeDtypeStruct((b, hrows, SSD_STATE), F32),
                   jax.ShapeDtypeStruct((b, 8, di), F32), jax.ShapeDtypeStruct((b, 8, di), F32)],
        scratch_shapes=[pltpu.VMEM((hrows, SSD_STATE), F32),
                        pltpu.VMEM((q + 8, di), F32), pltpu.VMEM((q + 8, di), F32),
                        pltpu.VMEM((q, di), F32), pltpu.VMEM((q, LANES), F32), pltpu.VMEM((nheads, q), F32),
                        pltpu.VMEM((q, d), F32)] + stage,
        compiler_params=_cparams(("parallel", "arbitrary")),
        name="ssd_mixer",
    )(x3, nwx, wzx, wdtp, wdt, brow, bcol, h0, tail, tail, cw, cw, cb, cb, arow, acol, drow, nw, ltri, utri)


def _attn_sample_kernel(pt_ref, q_ref, kn_ref, vn_ref, f_ref, ft_ref, bfr_ref, bfc_ref, u_ref, hm_ref,
                        *refs, pg, nq, nh, scale):
    k_refs, v_refs, lf_refs = refs[0:pg], refs[pg:2 * pg], refs[2 * pg:3 * pg]
    o_ref, lfo_ref = refs[3 * pg], refs[3 * pg + 1]
    qbd_sc, m_sc, l_sc, acc_sc, carry_sc, knp_sc, vnp_sc = refs[3 * pg + 2:]
    g = pl.program_id(1)
    rows = nq * nh
    d = acc_sc.shape[1]

    @pl.when(g == 0)
    def _():
        qv = q_ref[...] * scale
        hm = hm_ref[...]
        for qi in range(nq):
            qbd_sc[qi * nh:(qi + 1) * nh, :] = (qv[qi:qi + 1, :] * hm).astype(BF16)
        m_sc[...] = jnp.full(m_sc.shape, -jnp.inf, F32)
        l_sc[...] = jnp.zeros(l_sc.shape, F32)
        acc_sc[...] = jnp.zeros(acc_sc.shape, F32)
        carry_sc[...] = jnp.zeros(carry_sc.shape, F32)

    def update(s, pv_fn):
        m_prev = m_sc[...]
        m_new = jnp.maximum(m_prev, jnp.max(s, axis=-1, keepdims=True))
        alpha = jnp.exp(m_prev - m_new)
        p = jnp.exp(s - jnp.concatenate([m_new] * (s.shape[1] // LANES), axis=1))
        l_sc[...] = alpha * l_sc[...] + jnp.sum(p, axis=-1, keepdims=True)
        acc_sc[...] = jnp.concatenate([alpha] * (d // LANES), axis=1) * acc_sc[...] + pv_fn(p.astype(BF16))
        m_sc[...] = m_new

    qbd = qbd_sc[...]
    off = carry_sc[...]
    scores = []
    for pi in range(pg):
        loc = _dot_f32_lhs(lf_refs[pi][...], u_ref[...])
        cpage = loc + off
        off = off + loc[:, LANES - 1:LANES]
        kt = k_refs[pi][...].astype(BF16)
        scores.append(_dot(qbd, kt) - jnp.concatenate([cpage] * nq, axis=0))
    carry_sc[...] = off

    def pv_pages(p):
        acc = None
        for pi in range(pg):
            part = _dot(p[:, pi * LANES:(pi + 1) * LANES], v_refs[pi][...].astype(BF16), NT)
            acc = part if acc is None else acc + part
        return acc

    update(jnp.concatenate(scores, axis=1), pv_pages)

    @pl.when(g == pl.num_programs(1) - 1)
    def _():
        lfo_ref[...] = _log_sigmoid(f_ref[:, 0:nh] + bfr_ref[...])
        lft = _log_sigmoid(ft_ref[...] + bfc_ref[...])
        lane = lax.broadcasted_iota(jnp.int32, (nh, LANES), 1)
        cnew = jnp.zeros((nh, LANES), F32)
        run = carry_sc[...]
        for t in range(nq):
            run = run + lft[:, t:t + 1]
            cnew = jnp.where(lane == t, run, cnew)
        knp_sc[...] = jnp.zeros(knp_sc.shape, F32)
        knp_sc[0:nq, :] = kn_ref[...]
        vnp_sc[...] = jnp.zeros(vnp_sc.shape, F32)
        vnp_sc[0:nq, :] = vn_ref[...]
        s = _dot(qbd_sc[...], knp_sc[...].astype(BF16), NT) - jnp.concatenate([cnew] * nq, axis=0)
        key = lax.broadcasted_iota(jnp.int32, (rows, LANES), 1)
        ridx = lax.broadcasted_iota(jnp.int32, (rows, LANES), 0)
        qrow = jnp.zeros((rows, LANES), jnp.int32)
        for qi in range(1, nq):
            qrow = qrow + (ridx >= qi * nh).astype(jnp.int32)
        s = jnp.where(key <= qrow, s, -jnp.inf)
        vnb = vnp_sc[...].astype(BF16)
        update(s, lambda p: _dot(p, vnb))
        out = acc_sc[...] / l_sc[:, 0:1]
        hm = hm_ref[...]
        for qi in range(nq):
            o_ref[qi:qi + 1, :] = jnp.sum(out[qi * nh:(qi + 1) * nh, :] * hm, axis=0,
                                          keepdims=True).astype(o_ref.dtype)


def _attn_sample(page_table, proj3, ft3, bf_row, bf_col, utri, hmask, kt_cache, vt_cache, lf_cache,
                 *, d, pg=8):
    nseq, nq, _ = proj3.shape
    npages = page_table.shape[1]
    nh = hmask.shape[0]
    rows = nq * nh
    scale = HEAD_DIM ** -0.5
    page = kt_cache.shape[-1]
    fblk = (3 * d) // LANES

    def pmap(pi):
        return lambda b, g, pt: (pt[b, g * pg + pi], 0, 0, 0)

    cmap = lambda b, g, pt: (0, 0)
    in_specs = [pl.BlockSpec((None, nq, d), lambda b, g, pt: (b, 0, 0)),
                pl.BlockSpec((None, nq, d), lambda b, g, pt: (b, 0, 1)),
                pl.BlockSpec((None, nq, d), lambda b, g, pt: (b, 0, 2)),
                pl.BlockSpec((None, nq, LANES), lambda b, g, pt: (b, 0, fblk)),
                pl.BlockSpec((None, nh, nq), lambda b, g, pt: (b, 0, 0)),
                pl.BlockSpec((1, nh), cmap), pl.BlockSpec((nh, 1), cmap),
                pl.BlockSpec((page, page), cmap), pl.BlockSpec((nh, d), cmap)]
    in_specs += [pl.BlockSpec((None, None, d, page), pmap(pi)) for pi in range(pg)]
    in_specs += [pl.BlockSpec((None, None, d, page), pmap(pi)) for pi in range(pg)]
    in_specs += [pl.BlockSpec((None, None, nh, page), pmap(pi)) for pi in range(pg)]
    grid_spec = pltpu.PrefetchScalarGridSpec(
        num_scalar_prefetch=1,
        grid=(nseq, npages // pg),
        in_specs=in_specs,
        out_specs=[pl.BlockSpec((None, nq, d), lambda b, g, pt: (b, 0, 0)),
                   pl.BlockSpec((None, nq, nh), lambda b, g, pt: (b, 0, 0))],
        scratch_shapes=[pltpu.VMEM((rows, d), BF16), pltpu.VMEM((rows, LANES), F32),
                        pltpu.VMEM((rows, LANES), F32),
                        pltpu.VMEM((rows, d), F32), pltpu.VMEM((nh, 1), F32),
                        pltpu.VMEM((LANES, d), F32), pltpu.VMEM((LANES, d), F32)])
    kern = functools.partial(_attn_sample_kernel, pg=pg, nq=nq, nh=nh, scale=scale)
    return pl.pallas_call(
        kern,
        grid_spec=grid_spec,
        out_shape=[jax.ShapeDtypeStruct((nseq, nq, d), F32), jax.ShapeDtypeStruct((nseq, nq, nh), F32)],
        compiler_params=_cparams(("parallel", "arbitrary")),
        name="fox_attn_sample",
    )(page_table, proj3, proj3, proj3, proj3, ft3, bf_row, bf_col, utri, hmask,
      *([kt_cache] * pg), *([vt_cache] * pg), *([lf_cache] * pg))


def _tri_upper(n):
    r = lax.broadcasted_iota(jnp.int32, (n, n), 0)
    c = lax.broadcasted_iota(jnp.int32, (n, n), 1)
    return (r <= c).astype(BF16)


def kernel(x_prompt, x_sample, cache_k, cache_v, cache_logf, page_table, state_ssm, state_conv,
           norm_mix_w, norm_mlp_w, norm_out_w, fox_w_in, fox_b_f, fox_w_out,
           ssd_w_in, ssd_conv_w, ssd_conv_b, ssd_dt_bias, ssd_a_log, ssd_d, ssd_norm_w, ssd_w_out,
           mlp_w_up, mlp_w_down):
    bp, lp, d = x_prompt.shape
    bs, ls, _ = x_sample.shape
    nh = fox_b_f.shape[-1]
    nsh = ssd_dt_bias.shape[-1]
    di = nsh * HEAD_DIM
    nzx = ssd_w_in.shape[-1] - nsh
    page = cache_k.shape[2]

    fox_wt = jnp.transpose(fox_w_in[0]).astype(BF16)
    wq, wk, wv, wf = fox_wt[0:d], fox_wt[d:2 * d], fox_wt[2 * d:3 * d], fox_wt[3 * d:]
    npad = (-fox_wt.shape[0]) % LANES
    fox_wt_pad = jnp.concatenate([fox_wt, jnp.zeros((npad, d), BF16)], axis=0)
    bf_row = fox_b_f[0].reshape(1, nh)
    bf_col = fox_b_f[0].reshape(nh, 1)
    fox_wo = fox_w_out[0].astype(BF16)
    ssd_wt = jnp.transpose(ssd_w_in[0]).astype(BF16)
    wzx, wdt = ssd_wt[0:nzx], ssd_wt[nzx:]
    wdtp = jnp.concatenate([wdt, jnp.zeros((LANES - nsh, d), BF16)], axis=0)
    dtb_row = jnp.concatenate([ssd_dt_bias[0], jnp.zeros((LANES - nsh,), F32)]).reshape(1, LANES)
    dtb_col = ssd_dt_bias[0].reshape(nsh, 1)
    a_neg = -jnp.exp(ssd_a_log[0])
    a_row = jnp.concatenate([a_neg, jnp.zeros((LANES - nsh,), F32)]).reshape(1, LANES)
    a_col = a_neg.reshape(nsh, 1)
    d_row = jnp.repeat(ssd_d[0], HEAD_DIM).reshape(1, di)
    ssd_nw = ssd_norm_w[0].reshape(1, di)
    ssd_wo = ssd_w_out[0].astype(BF16)
    conv_w = ssd_conv_w[0]
    conv_b = ssd_conv_b[0].reshape(1, -1)
    wup = mlp_w_up.astype(BF16)
    wdn = mlp_w_down.astype(BF16)
    nmix = norm_mix_w.reshape(-1, 1, d)
    nmlp = norm_mlp_w.reshape(-1, 1, d)
    nout = norm_out_w.reshape(1, d)
    utri = _tri_upper(LANES)
    ltri = jnp.transpose(utri)
    utri_c = _tri_upper(256)
    hmask = (lax.broadcasted_iota(jnp.int32, (nh, d), 1) // HEAD_DIM
             == lax.broadcasted_iota(jnp.int32, (nh, d), 0)).astype(F32)

    def ssd_layer(x, batch, seq, h0, tail):
        gy, hlast, tx, tbc = _ssd_mixer(x.reshape(batch, seq, d), nmix[1], wzx, wdtp, wdt, dtb_row, dtb_col,
                                        h0, tail, conv_w, conv_b, a_row, a_col, d_row, ssd_nw, ltri, utri)
        x = _linear_residual(gy.reshape(batch * seq, di), ssd_wo, x)
        y = _mlp_block(x, nmlp[1], wup[1], wdn[1], nout, final_norm=True)
        new_conv = jnp.concatenate([tx[:, 5:], tbc[:, 5:]], axis=-1).reshape(batch, 1, 3, nzx - di)
        return y, hlast.reshape(batch, 1, nsh, HEAD_DIM, SSD_STATE), new_conv

    tp = bp * lp
    xp = x_prompt.reshape(tp, d)
    q, kt, vt, ktb, vtb, lft = _fox_in_prompt(xp, nmix[0], wq, wk, wv, wf, bf_col, batch=bp, seq=lp)
    ct = _cumsum_lanes(lft, utri_c)
    o = _attn_prompt(q, ktb, vtb, ct.reshape(bp, nh // 2, 2, lp), batch=bp, seq=lp)
    xp = _linear_residual(o, fox_wo, xp)
    xp = _mlp_block(xp, nmlp[0], wup[0], wdn[0], nout, final_norm=False)
    h0_p = jnp.zeros((bp, nsh * HEAD_DIM, SSD_STATE), state_ssm.dtype)
    tail_p = jnp.zeros((bp, 8, nzx - di), F32)
    y_p, ssm_p, conv_p = ssd_layer(xp, bp, lp, h0_p, tail_p)
    y_prompt = y_p.reshape(bp, lp, d)
    new_k_prompt = jnp.transpose(kt.reshape(bp, 1, nh, HEAD_DIM, lp), (0, 1, 4, 2, 3))
    new_v_prompt = jnp.transpose(vt.reshape(bp, 1, nh, HEAD_DIM, lp), (0, 1, 4, 2, 3))
    new_logf_prompt = jnp.transpose(lft.reshape(bp, 1, nh, lp), (0, 1, 3, 2))

    ts = bs * ls
    xs = x_sample.reshape(ts, d)
    proj = _norm_linear(xs, nmix[0], fox_wt_pad)
    proj3 = proj.reshape(bs, ls, -1)
    ft3 = jnp.transpose(proj3[:, :, 3 * d:3 * d + nh], (0, 2, 1))
    kt_cache = jnp.transpose(cache_k, (0, 1, 3, 4, 2)).reshape(cache_k.shape[0], cache_k.shape[1], d, page)
    vt_cache = jnp.transpose(cache_v, (0, 1, 3, 4, 2)).reshape(cache_v.shape[0], cache_v.shape[1], d, page)
    lf_cache = jnp.transpose(cache_logf, (0, 1, 3, 2))
    o_s, lf_s = _attn_sample(page_table, proj3, ft3, bf_row, bf_col, utri, hmask,
                             kt_cache, vt_cache, lf_cache, d=d)
    xs = _linear_residual(o_s.reshape(ts, d), fox_wo, xs)
    xs = _mlp_block(xs, nmlp[0], wup[0], wdn[0], nout, final_norm=False)
    h0_s = state_ssm[:, 0].reshape(bs, nsh * HEAD_DIM, SSD_STATE)
    tail_s = jnp.concatenate([jnp.zeros((bs, 5, nzx - di), F32), state_conv[:, 0]], axis=1)
    y_s, ssm_s, conv_s = ssd_layer(xs, bs, ls, h0_s, tail_s)
    y_sample = y_s.reshape(bs, ls, d)
    new_k_sample = proj3[:, :, d:2 * d].reshape(bs, 1, ls, nh, HEAD_DIM)
    new_v_sample = proj3[:, :, 2 * d:3 * d].reshape(bs, 1, ls, nh, HEAD_DIM)
    new_logf_sample = lf_s.reshape(bs, 1, ls, nh)

    return (y_prompt, y_sample, new_k_prompt, new_v_prompt, new_logf_prompt, ssm_p, conv_p,
            new_k_sample, new_v_sample, new_logf_sample, ssm_s, conv_s)
```

```python
import functools

import jax
import jax.numpy as jnp
from jax import lax
from jax.experimental import pallas as pl
from jax.experimental.pallas import tpu as pltpu

F32 = jnp.float32
BF16 = jnp.bfloat16
EPS = 1e-5
LANES = 128
HEAD_DIM = 64
SSD_STATE = 128
SSD_GROUP_W = 256
VMEM_LIMIT = 52 * 1024 * 1024

NT = (((1,), (1,)), ((), ()))
NN = (((1,), (0,)), ((), ()))
TN = (((0,), (0,)), ((), ()))


def _cparams(sem):
    return pltpu.CompilerParams(dimension_semantics=sem, vmem_limit_bytes=VMEM_LIMIT)


def _dot(a, b, dims=NN):
    return lax.dot_general(a, b, dims, preferred_element_type=F32)


def _split3(x):
    hi = x.astype(BF16)
    r = x - hi.astype(F32)
    mid = r.astype(BF16)
    lo = (r - mid.astype(F32)).astype(BF16)
    return hi, mid, lo


def _dot_f32_lhs(x, sel, dims=NN):
    hi, mid, lo = _split3(x)
    return _dot(hi, sel, dims) + _dot(mid, sel, dims) + _dot(lo, sel, dims)


def _dot_f32_rhs(sel, x, dims=NN):
    hi, mid, lo = _split3(x)
    return _dot(sel, hi, dims) + _dot(sel, mid, dims) + _dot(sel, lo, dims)


def _rmsnorm(x, w):
    ms = jnp.mean(x * x, axis=-1, keepdims=True)
    return x * lax.rsqrt(ms + EPS) * w


def _softplus(x):
    return jnp.maximum(x, 0.0) + jnp.log1p(jnp.exp(-jnp.abs(x)))


def _log_sigmoid(x):
    return -_softplus(-x)


def _silu(x):
    hx = 0.5 * x
    return hx + hx * jnp.tanh(hx)


def _fox_in_prompt_kernel(x_ref, nw_ref, wq_ref, wk_ref, wv_ref, wf_ref, bf_ref,
                          q_ref, kt_ref, vt_ref, ktb_ref, vtb_ref, lft_ref, *, scale, nchunk):
    xn = _rmsnorm(x_ref[...], nw_ref[...]).astype(BF16)
    d = xn.shape[1]
    cw = d // nchunk
    for c in range(nchunk):
        sl = slice(c * cw, (c + 1) * cw)
        q = _dot(xn, wq_ref[sl, :], NT)
        q_ref[:, sl] = (q * scale).astype(BF16)
        kt = _dot(wk_ref[sl, :], xn, NT)
        kt_ref[0, sl, :] = kt
        ktb_ref[0, sl, :] = kt.astype(BF16)
        vt = _dot(wv_ref[sl, :], xn, NT)
        vt_ref[0, sl, :] = vt
        vtb_ref[0, sl, :] = vt.astype(BF16)
    ft = _dot(wf_ref[...], xn, NT)
    lft_ref[0] = _log_sigmoid(ft + bf_ref[...])


def _fox_in_prompt(x, nw, wq, wk, wv, wf, bf_col, *, batch, seq, tm=512):
    t, d = x.shape
    nh = wf.shape[0]
    nt = seq // tm
    scale = HEAD_DIM ** -0.5
    row = lambda i: (i, 0)
    full = lambda i: (0, 0)
    tr = lambda i: (i // nt, 0, i % nt)
    kern = functools.partial(_fox_in_prompt_kernel, scale=scale, nchunk=4)
    return pl.pallas_call(
        kern,
        grid=(t // tm,),
        in_specs=[pl.BlockSpec((tm, d), row), pl.BlockSpec((1, d), full),
                  pl.BlockSpec((d, d), full), pl.BlockSpec((d, d), full), pl.BlockSpec((d, d), full),
                  pl.BlockSpec((nh, d), full), pl.BlockSpec((nh, 1), full)],
        out_specs=[pl.BlockSpec((tm, d), row),
                   pl.BlockSpec((1, d, tm), tr), pl.BlockSpec((1, d, tm), tr),
                   pl.BlockSpec((1, d, tm), tr), pl.BlockSpec((1, d, tm), tr),
                   pl.BlockSpec((1, nh, tm), tr)],
        out_shape=[jax.ShapeDtypeStruct((t, d), BF16),
                   jax.ShapeDtypeStruct((batch, d, seq), F32), jax.ShapeDtypeStruct((batch, d, seq), F32),
                   jax.ShapeDtypeStruct((batch, d, seq), BF16), jax.ShapeDtypeStruct((batch, d, seq), BF16),
                   jax.ShapeDtypeStruct((batch, nh, seq), F32)],
        compiler_params=_cparams(("parallel",)),
        name="fox_in_prompt",
    )(x, nw, wq, wk, wv, wf, bf_col)


def _cumsum_kernel(lf_ref, u_ref, c_ref, *, chunk):
    nh, seq = lf_ref.shape[1], lf_ref.shape[2]
    carry = jnp.zeros((nh, 1), F32)
    for j in range(seq // chunk):
        sl = slice(j * chunk, (j + 1) * chunk)
        c = _dot_f32_lhs(lf_ref[0, :, sl], u_ref[...]) + carry
        c_ref[0, :, sl] = c
        carry = c[:, chunk - 1:chunk]


def _cumsum_lanes(lft, utri):
    b, nh, seq = lft.shape
    chunk = utri.shape[0]
    return pl.pallas_call(
        functools.partial(_cumsum_kernel, chunk=chunk),
        grid=(b,),
        in_specs=[pl.BlockSpec((1, nh, seq), lambda i: (i, 0, 0)),
                  pl.BlockSpec((chunk, chunk), lambda i: (0, 0))],
        out_specs=pl.BlockSpec((1, nh, seq), lambda i: (i, 0, 0)),
        out_shape=jax.ShapeDtypeStruct((b, nh, seq), F32),
        compiler_params=_cparams(("parallel",)),
        name="fox_cumsum",
    )(lft, utri)


N_SPLIT = 3


def _attn_prompt_kernel(q_ref, kt_ref, vt_ref, c_ref, o_ref, m_sc, acc_sc, *, blk, wide, dsplit):
    i = pl.program_id(2)
    q2 = q_ref[...].astype(F32)
    lane = lax.broadcasted_iota(jnp.int32, (blk, LANES), 1)
    lo = lane < HEAD_DIM
    qe = [jnp.where(lo, q2, jnp.where(lane < HEAD_DIM + N_SPLIT, 1.0, 0.0)).astype(BF16),
          jnp.where(lo, jnp.where(lane < N_SPLIT, 1.0, 0.0), q2).astype(BF16)]
    m_sc[...] = jnp.full(m_sc.shape, -jnp.inf, F32)
    acc_sc[...] = jnp.zeros(acc_sc.shape, F32)
    hb = blk // dsplit
    consts = {}
    for wk in sorted({blk, wide * blk} | {(t + 1) * hb for t in range(dsplit)}):
        rowi = lax.broadcasted_iota(jnp.int32, (16, wk), 0)
        pad = jnp.zeros((HEAD_DIM - 16, wk), BF16)
        ones_blk = jnp.concatenate([jnp.where(rowi == 0, 1.0, 0.0).astype(BF16), pad], axis=0)
        consts[wk] = (rowi, pad, ones_blk)

    def step(off, wk, r0=0, nr=blk, shift=None):
        rowi, pad, ones_blk = consts[wk]
        nrep = wk // LANES
        rows = slice(r0, r0 + nr)
        ck = c_ref[0, 0, :, pl.ds(off, wk)]
        if shift is not None:
            r = lax.broadcasted_iota(jnp.int32, (nr, wk), 0)
            cidx = lax.broadcasted_iota(jnp.int32, (nr, wk), 1)
            vis = cidx <= r + shift
        scores, vtes = [], []
        for h in range(2):
            hs = slice(h * HEAD_DIM, (h + 1) * HEAD_DIM)
            hi, mid, low = _split3(-ck[h:h + 1, :])
            b16 = jnp.where(rowi == 0, hi.astype(F32),
                            jnp.where(rowi == 1, mid.astype(F32),
                                      jnp.where(rowi == 2, low.astype(F32), 0.0))).astype(BF16)
            bias_blk = jnp.concatenate([b16, pad], axis=0)
            kth = kt_ref[0, hs, pl.ds(off, wk)]
            vth = vt_ref[0, hs, pl.ds(off, wk)]
            if h == 0:
                kte = jnp.concatenate([kth, bias_blk], axis=0)
                vtes.append(jnp.concatenate([vth, ones_blk], axis=0))
            else:
                kte = jnp.concatenate([bias_blk, kth], axis=0)
                vtes.append(jnp.concatenate([ones_blk, vth], axis=0))
            s = _dot(qe[h][rows], kte)
            if shift is not None:
                s = jnp.where(vis, s, -jnp.inf)
            scores.append(s)
        probs, alphas = [], []
        for h in range(2):
            m_prev = m_sc[h, rows, :]
            m_new = jnp.maximum(m_prev, jnp.max(scores[h], axis=-1, keepdims=True))
            alphas.append(jnp.exp(m_prev - m_new))
            probs.append(jnp.exp(scores[h] - jnp.concatenate([m_new] * nrep, axis=1)).astype(BF16))
            m_sc[h, rows, :] = m_new
        for h in range(2):
            pv = _dot(probs[h], vtes[h], NT)
            acc_sc[h, rows, :] = alphas[h] * acc_sc[h, rows, :] + pv

    def body(j, carry):
        step(pl.multiple_of(j * (wide * blk), wide * blk), wide * blk)
        return carry

    nwide = i // wide
    lax.fori_loop(0, nwide, body, 0)
    for extra in range(wide - 1):
        @pl.when(nwide * wide + extra < i)
        def _():
            step(pl.multiple_of((nwide * wide + extra) * blk, blk), blk)
    for t in range(dsplit):
        step(pl.multiple_of(i * blk, blk), (t + 1) * hb, r0=t * hb, nr=hb, shift=t * hb)
    a0 = acc_sc[0]
    a1 = acc_sc[1]
    out0 = a0 / a0[:, HEAD_DIM:HEAD_DIM + 1]
    out1 = a1 / a1[:, 0:1]
    o_ref[...] = jnp.where(lo, out0, out1).astype(o_ref.dtype)


def _attn_prompt(q, ktb, vtb, c4, *, batch, seq, blk=512, wide=2, dsplit=1):
    t, d = q.shape
    npair = d // LANES
    nq = seq // blk
    qmap = lambda b, h, i: (b * nq + i, h)
    kmap = lambda b, h, i: (b, h, 0)
    return pl.pallas_call(
        functools.partial(_attn_prompt_kernel, blk=blk, wide=wide, dsplit=dsplit),
        grid=(batch, npair, nq),
        in_specs=[pl.BlockSpec((blk, LANES), qmap),
                  pl.BlockSpec((1, LANES, seq), kmap), pl.BlockSpec((1, LANES, seq), kmap),
                  pl.BlockSpec((1, 1, 2, seq), lambda b, h, i: (b, h, 0, 0))],
        out_specs=pl.BlockSpec((blk, LANES), qmap),
        out_shape=jax.ShapeDtypeStruct((t, d), BF16),
        scratch_shapes=[pltpu.VMEM((2, blk, LANES), F32), pltpu.VMEM((2, blk, LANES), F32)],
        compiler_params=_cparams(("parallel", "parallel", "arbitrary")),
        name="fox_attn_prompt",
    )(q, ktb, vtb, c4)


def _linear_residual_kernel(a_ref, w_ref, x_ref, o_ref):
    o_ref[...] = x_ref[...] + _dot(a_ref[...].astype(BF16), w_ref[...])


def _linear_residual(a, w, x, *, tm=512):
    t, k = a.shape
    n = w.shape[1]
    tm = min(tm, t)
    return pl.pallas_call(
        _linear_residual_kernel,
        grid=(t // tm,),
        in_specs=[pl.BlockSpec((tm, k), lambda i: (i, 0)), pl.BlockSpec((k, n), lambda i: (0, 0)),
                  pl.BlockSpec((tm, n), lambda i: (i, 0))],
        out_specs=pl.BlockSpec((tm, n), lambda i: (i, 0)),
        out_shape=jax.ShapeDtypeStruct((t, n), F32),
        compiler_params=_cparams(("parallel",)),
        name="linear_residual",
    )(a, w, x)


def _mlp_kernel(x_ref, nw_ref, wup_ref, wdn_ref, fw_ref, o_ref, xn_sc, acc_sc, *, final_norm):
    f = pl.program_id(1)

    @pl.when(f == 0)
    def _():
        x = x_ref[...]
        xn_sc[...] = _rmsnorm(x, nw_ref[...]).astype(BF16)
        acc_sc[...] = x

    h = jnp.maximum(_dot(xn_sc[...], wup_ref[...]), 0.0)
    acc_sc[...] += _dot((h * h).astype(BF16), wdn_ref[...])

    @pl.when(f == pl.num_programs(1) - 1)
    def _():
        y = acc_sc[...]
        if final_norm:
            y = _rmsnorm(y, fw_ref[...])
        o_ref[...] = y


def _mlp_block(x, nw, wup, wdn, fw, *, final_norm, tm=1024, tf=1024):
    t, d = x.shape
    dff = wup.shape[1]
    tm = min(tm, t)
    return pl.pallas_call(
        functools.partial(_mlp_kernel, final_norm=final_norm),
        grid=(t // tm, dff // tf),
        in_specs=[pl.BlockSpec((tm, d), lambda i, f: (i, 0)), pl.BlockSpec((1, d), lambda i, f: (0, 0)),
                  pl.BlockSpec((d, tf), lambda i, f: (0, f)), pl.BlockSpec((tf, d), lambda i, f: (f, 0)),
                  pl.BlockSpec((1, d), lambda i, f: (0, 0))],
        out_specs=pl.BlockSpec((tm, d), lambda i, f: (i, 0)),
        out_shape=jax.ShapeDtypeStruct((t, d), F32),
        scratch_shapes=[pltpu.VMEM((tm, d), BF16), pltpu.VMEM((tm, d), F32)],
        compiler_params=_cparams(("parallel", "arbitrary")),
        name="mlp_block",
    )(x, nw, wup, wdn, fw)


def _norm_linear_kernel(x_ref, nw_ref, wt_ref, o_ref):
    xn = _rmsnorm(x_ref[...], nw_ref[...]).astype(BF16)
    o_ref[...] = _dot(xn, wt_ref[...], NT)


def _norm_linear(x, nw, wt, *, tm=128):
    t, d = x.shape
    n = wt.shape[0]
    return pl.pallas_call(
        _norm_linear_kernel,
        grid=(t // tm,),
        in_specs=[pl.BlockSpec((tm, d), lambda i: (i, 0)), pl.BlockSpec((1, d), lambda i: (0, 0)),
                  pl.BlockSpec((n, d), lambda i: (0, 0))],
        out_specs=pl.BlockSpec((tm, n), lambda i: (i, 0)),
        out_shape=jax.ShapeDtypeStruct((t, n), F32),
        compiler_params=_cparams(("parallel",)),
        name="norm_linear",
    )(x, nw, wt)


def _ssd_mixer_kernel(*refs, q, valid, nheads, ncol, pipe, preproj):
    nlead = 5 if preproj else 7
    (h0_ref, tx_ref, tbc_ref,
     cwx_ref, cwbc_ref, cbx_ref, cbbc_ref, arow_ref, acol_ref, drow_ref, nw_ref,
     ltri_ref, utri_ref,
     gy_ref, hout_ref, txo_ref, tbco_ref,
     h_sc, xpad_sc, bcpad_sc, z_sc, dtn_sc, dtt_sc, x_sc,
     sz_sc, sx_sc, sbc_sc, sdtn_sc, sdtt_sc) = refs[nlead:]
    c = pl.program_id(1)
    di = z_sc.shape[1]
    ngroups = di // SSD_GROUP_W
    first_scan = 1 if pipe else 0

    @pl.when(c == 0)
    def _():
        if pipe:
            for ref in (sz_sc, sx_sc, sbc_sc, sdtn_sc, sdtt_sc):
                ref[...] = jnp.zeros(ref.shape, F32)
        xpad_sc[0:8, :] = tx_ref[...]
        bcpad_sc[0:8, :] = tbc_ref[...]

    @pl.when(c <= first_scan)
    def _():
        h_sc[...] = h0_ref[...]

    if pipe:
        z_sc[...] = sz_sc[...]
        xpad_sc[8:8 + q, :] = sx_sc[...]
        bcpad_sc[8:8 + q, :] = sbc_sc[...]
        dtn_sc[...] = sdtn_sc[...]
        dtt_sc[...] = sdtt_sc[...]
        pz, px, pbc, pdtn, pdtt, prow = sz_sc, sx_sc, sbc_sc, sdtn_sc, sdtt_sc, 0
    else:
        pz, px, pbc, pdtn, pdtt, prow = z_sc, xpad_sc, bcpad_sc, dtn_sc, dtt_sc, 8

    if preproj:
        zin_ref, xin_ref, bcin_ref, dtnin_ref, dttin_ref = refs[:nlead]
        assert not pipe
        for dst, src, r0 in ((z_sc, zin_ref, 0), (xpad_sc, xin_ref, 8), (bcpad_sc, bcin_ref, 8),
                             (dtn_sc, dtnin_ref, 0)):
            if valid != q:
                dst[r0:r0 + q, :] = jnp.zeros((q, dst.shape[1]), F32)
            dst[r0:r0 + valid, :] = src[...]
        if valid != q:
            dtt_sc[...] = jnp.zeros(dtt_sc.shape, F32)
        dtt_sc[:, 0:valid] = dttin_ref[...]
    else:
        x_ref, nwx_ref, wzx_ref, wdtp_ref, wdt_ref, brow_ref, bcol_ref = refs[:nlead]
        if valid == q:
            x = x_ref[...]
        else:
            x_sc[...] = jnp.zeros(x_sc.shape, F32)
            x_sc[0:valid, :] = x_ref[...]
            x = x_sc[...]
        xn = _rmsnorm(x, nwx_ref[...]).astype(BF16)
        dtn_p = _softplus(_dot(xn, wdtp_ref[...], NT) + brow_ref[...])
        dtt_p = _softplus(_dot(wdt_ref[...], xn, NT) + bcol_ref[...])
        if valid != q:
            dtn_p = jnp.where(lax.broadcasted_iota(jnp.int32, dtn_p.shape, 0) < valid, dtn_p, 0.0)
            dtt_p = jnp.where(lax.broadcasted_iota(jnp.int32, dtt_p.shape, 1) < valid, dtt_p, 0.0)
        pdtn[...] = dtn_p
        pdtt[...] = dtt_p
        cw = (3 * di) // ncol

        def project(k):
            piece = _dot(xn, wzx_ref[k * cw:(k + 1) * cw, :], NT)
            tgt, off = divmod(k * cw, di)
            if tgt == 0:
                pz[:, off:off + cw] = piece
            elif tgt == 1:
                px[prow:prow + q, off:off + cw] = piece
            else:
                pbc[prow:prow + q, off:off + cw] = piece

        if not pipe:
            for k in range(ncol):
                project(k)

    def conv(pad_sc, w_ref, b_ref, sl):
        w = w_ref[:, sl]
        acc = b_ref[:, sl] + pad_sc[5:5 + q, sl] * w[0:1, :]
        acc = acc + pad_sc[6:6 + q, sl] * w[1:2, :]
        acc = acc + pad_sc[7:7 + q, sl] * w[2:3, :]
        acc = acc + pad_sc[8:8 + q, sl] * w[3:4, :]
        return _silu(acc)

    dtn = dtn_sc[...]
    dtt = dtt_sc[...]
    a_nat = dtn * arow_ref[...]
    a_t = dtt * acol_ref[...]
    cum_nat = _dot_f32_rhs(ltri_ref[...], a_nat)
    cum_t = _dot_f32_lhs(a_t, utri_ref[...])
    ri = lax.broadcasted_iota(jnp.int32, (q, q), 0)
    ci = lax.broadcasted_iota(jnp.int32, (q, q), 1)
    tri = ci <= ri
    lo = lax.broadcasted_iota(jnp.int32, (q, LANES), 1) < HEAD_DIM
    rlo = lax.broadcasted_iota(jnp.int32, (LANES, SSD_STATE), 0) < HEAD_DIM
    nb = di // 2
    clast_row = cum_nat[q - 1:q, :]
    e_nat = jnp.exp(cum_nat)
    w_nat = jnp.exp(clast_row - cum_nat) * dtn
    dl_row = jnp.exp(clast_row)

    for g in range(ngroups):
        if pipe:
            for k in range(g * ncol // ngroups, (g + 1) * ncol // ngroups):
                project(k)
        bg = conv(bcpad_sc, cwbc_ref, cbbc_ref, slice(g * SSD_STATE, (g + 1) * SSD_STATE))
        cg = conv(bcpad_sc, cwbc_ref, cbbc_ref, slice(nb + g * SSD_STATE, nb + (g + 1) * SSD_STATE))
        bb = bg.astype(BF16)
        cb16 = cg.astype(BF16)
        cb = _dot(cb16, bb, NT)
        gated = []
        ssq = jnp.zeros((q, 1), F32)
        for pr in range(2):
            p = 2 * g + pr
            sl = slice(p * LANES, (p + 1) * LANES)
            xp = conv(xpad_sc, cwx_ref, cbx_ref, sl)
            xpb = xp.astype(BF16)
            yi, ee, we, dl = [], [], [], []
            for r in range(2):
                h = 2 * p + r
                colb = jnp.broadcast_to(cum_nat[:, h:h + 1], (q, LANES))
                rowb = cum_t[h:h + 1, :]
                dec = jnp.exp(jnp.where(tri, colb - rowb, -jnp.inf))
                w = (cb * dec * dtt[h:h + 1, :]).astype(BF16)
                yi.append(_dot(w, xpb))
                ee.append(jnp.broadcast_to(e_nat[:, h:h + 1], (q, LANES)))
                we.append(jnp.broadcast_to(w_nat[:, h:h + 1], (q, LANES)))
                dl.append(jnp.broadcast_to(dl_row[:, h:h + 1], (LANES, SSD_STATE)))
            hp = h_sc[sl, :]
            y_inter = _dot(cb16, hp.astype(BF16), NT) * jnp.where(lo, ee[0], ee[1])
            xw = (xp * jnp.where(lo, we[0], we[1])).astype(BF16)
            h_sc[sl, :] = jnp.where(rlo, dl[0], dl[1]) * hp + _dot(xw, bb, TN)
            y = jnp.where(lo, yi[0], yi[1]) + y_inter + xp * drow_ref[:, sl]
            gt = y * _silu(z_sc[:, sl])
            ssq = ssq + jnp.sum(gt * gt, axis=-1, keepdims=True)
            gated.append(gt)
        rs = lax.rsqrt(ssq / SSD_GROUP_W + EPS)
        for pr in range(2):
            sl = slice((2 * g + pr) * LANES, (2 * g + pr + 1) * LANES)
            gy_ref[:, sl] = (gated[pr] * rs * nw_ref[:, sl])[0:valid].astype(gy_ref.dtype)

    @pl.when(c == pl.num_programs(1) - 1)
    def _():
        hout_ref[...] = h_sc[...]
        txo_ref[...] = xpad_sc[valid:valid + 8, :]
        tbco_ref[...] = bcpad_sc[valid:valid + 8, :]

    @pl.when(c >= first_scan)
    def _():
        xpad_sc[0:8, :] = xpad_sc[q:q + 8, :]
        bcpad_sc[0:8, :] = bcpad_sc[q:q + 8, :]


def _ssd_proj_kernel(x_ref, nw_ref, w_ref, wdtp_ref, wdt_ref, brow_ref, bcol_ref, zx_ref, dtn_ref, dtt_ref):
    xn = _rmsnorm(x_ref[...], nw_ref[...]).astype(BF16)
    dtn_ref[...] = _softplus(_dot(xn, wdtp_ref[...], NT) + brow_ref[...])
    dtt_ref[...] = _softplus(_dot(wdt_ref[...], xn, NT) + bcol_ref[...])
    zx_ref[...] = _dot(xn, w_ref[...], NT)


def _ssd_proj(x, nw, wzx, wdtp, wdt, brow, bcol, *, tn=1024):
    t, d = x.shape
    n = wzx.shape[0]
    nh = wdt.shape[0]
    c0 = lambda j: (0, 0)
    return pl.pallas_call(
        _ssd_proj_kernel,
        grid=(n // tn,),
        in_specs=[pl.BlockSpec((t, d), c0), pl.BlockSpec((1, d), c0), pl.BlockSpec((tn, d), lambda j: (j, 0)),
                  pl.BlockSpec((LANES, d), c0), pl.BlockSpec((nh, d), c0),
                  pl.BlockSpec((1, LANES), c0), pl.BlockSpec((nh, 1), c0)],
        out_specs=[pl.BlockSpec((t, tn), lambda j: (0, j)), pl.BlockSpec((t, LANES), c0),
                   pl.BlockSpec((nh, t), c0)],
        out_shape=[jax.ShapeDtypeStruct((t, n), F32), jax.ShapeDtypeStruct((t, LANES), F32),
                   jax.ShapeDtypeStruct((nh, t), F32)],
        compiler_params=_cparams(("arbitrary",)),
        name="ssd_proj",
    )(x, nw, wzx, wdtp, wdt, brow, bcol)


def _ssd_mixer(x3, nwx, wzx, wdtp, wdt, brow, bcol, h0, tail, cw, cb, arow, acol, drow, nw, ltri, utri,
               *, q=128, ncol=6):
    b, seq, d = x3.shape
    nheads = wdt.shape[0]
    di = nheads * HEAD_DIM
    valid = min(q, seq)
    nc = max(seq // q, 1)
    pipe = nc > 1
    preproj = not pipe
    nsteps = nc + 1 if pipe else nc
    hrows = nheads * HEAD_DIM
    const2 = lambda k: (lambda i, c: (0, k))
    stage = [pltpu.VMEM((q, di), F32)] * 3 + [pltpu.VMEM((q, LANES), F32), pltpu.VMEM((nheads, q), F32)]
    if not pipe:
        stage = [pltpu.VMEM((8, LANES), F32)] * 5
    if pipe:
        ncol = 3 * (di // SSD_GROUP_W)
    if preproj:
        zx, dtn, dtt = _ssd_proj(x3.reshape(b * seq, d), nwx, wzx, wdtp, wdt, brow, bcol)
        zx3 = zx.reshape(b, seq, 3 * di)
        lead = [zx3, zx3, zx3, dtn.reshape(b, seq, LANES),
                jnp.transpose(dtt.reshape(nheads, b, seq), (1, 0, 2))]
        lead_specs = [pl.BlockSpec((None, valid, di), lambda i, c: (i, c, 0)),
                      pl.BlockSpec((None, valid, di), lambda i, c: (i, c, 1)),
                      pl.BlockSpec((None, valid, di), lambda i, c: (i, c, 2)),
                      pl.BlockSpec((None, valid, LANES), lambda i, c: (i, c, 0)),
                      pl.BlockSpec((None, nheads, valid), lambda i, c: (i, 0, c))]
    else:
        lead = [x3, nwx, wzx, wdtp, wdt, brow, bcol]
        lead_specs = [pl.BlockSpec((None, valid, d), lambda i, c: (i, jnp.minimum(c, nc - 1), 0)),
                      pl.BlockSpec((1, d), const2(0)), pl.BlockSpec((3 * di, d), const2(0)),
                      pl.BlockSpec((LANES, d), const2(0)), pl.BlockSpec((nheads, d), const2(0)),
                      pl.BlockSpec((1, LANES), const2(0)), pl.BlockSpec((nheads, 1), const2(0))]
    kern = functools.partial(_ssd_mixer_kernel, q=q, valid=valid, nheads=nheads, ncol=ncol, pipe=pipe,
                             preproj=preproj)
    return pl.pallas_call(
        kern,
        grid=(b, nsteps),
        in_specs=lead_specs + [
                  pl.BlockSpec((None, hrows, SSD_STATE), lambda i, c: (i, 0, 0)),
                  pl.BlockSpec((None, 8, di), lambda i, c: (i, 0, 0)),
                  pl.BlockSpec((None, 8, di), lambda i, c: (i, 0, 1)),
                  pl.BlockSpec((4, di), const2(0)), pl.BlockSpec((4, di), const2(1)),
                  pl.BlockSpec((1, di), const2(0)), pl.BlockSpec((1, di), const2(1)),
                  pl.BlockSpec((1, LANES), const2(0)), pl.BlockSpec((nheads, 1), const2(0)),
                  pl.BlockSpec((1, di), const2(0)), pl.BlockSpec((1, di), const2(0)),
                  pl.BlockSpec((q, q), const2(0)), pl.BlockSpec((q, q), const2(0))],
        out_specs=[pl.BlockSpec((None, valid, di), lambda i, c: (i, jnp.maximum(c - (nsteps - nc), 0), 0)),
                   pl.BlockSpec((None, hrows, SSD_STATE), lambda i, c: (i, 0, 0)),
                   pl.BlockSpec((None, 8, di), lambda i, c: (i, 0, 0)),
                   pl.BlockSpec((None, 8, di), lambda i, c: (i, 0, 0))],
        out_shape=[jax.ShapeDtypeStruct((b, seq, di), BF16 if valid == q else F32),
                   jax.ShapeDtypeStruct((b, hrows, SSD_STATE), F32),
                   jax.ShapeDtypeStruct((b, 8, di), F32), jax.ShapeDtypeStruct((b, 8, di), F32)],
        scratch_shapes=[pltpu.VMEM((hrows, SSD_STATE), F32),
                        pltpu.VMEM((q + 8, di), F32), pltpu.VMEM((q + 8, di), F32),
                        pltpu.VMEM((q, di), F32), pltpu.VMEM((q, LANES), F32), pltpu.VMEM((nheads, q), F32),
                        pltpu.VMEM((q, d), F32)] + stage,
        compiler_params=_cparams(("parallel", "arbitrary")),
        name="ssd_mixer",
    )(*lead, h0, tail, tail, cw, cw, cb, cb, arow, acol, drow, nw, ltri, utri)


def _attn_sample_kernel(pt_ref, q_ref, kn_ref, vn_ref, f_ref, ft_ref, bfr_ref, bfc_ref, u_ref, hm_ref,
                        *refs, pg, nq, nh, scale):
    k_refs, v_refs, lf_refs = refs[0:pg], refs[pg:2 * pg], refs[2 * pg:3 * pg]
    o_ref, lfo_ref = refs[3 * pg], refs[3 * pg + 1]
    qbd_sc, m_sc, l_sc, acc_sc, carry_sc, knp_sc, vnp_sc = refs[3 * pg + 2:]
    g = pl.program_id(1)
    rows = nq * nh
    d = acc_sc.shape[1]

    @pl.when(g == 0)
    def _():
        qv = q_ref[...] * scale
        hm = hm_ref[...]
        for qi in range(nq):
            qbd_sc[qi * nh:(qi + 1) * nh, :] = (qv[qi:qi + 1, :] * hm).astype(BF16)
        m_sc[...] = jnp.full(m_sc.shape, -jnp.inf, F32)
        l_sc[...] = jnp.zeros(l_sc.shape, F32)
        acc_sc[...] = jnp.zeros(acc_sc.shape, F32)
        carry_sc[...] = jnp.zeros(carry_sc.shape, F32)

    def update(s, pv_fn):
        m_prev = m_sc[...]
        m_new = jnp.maximum(m_prev, jnp.max(s, axis=-1, keepdims=True))
        alpha = jnp.exp(m_prev - m_new)
        p = jnp.exp(s - jnp.concatenate([m_new] * (s.shape[1] // LANES), axis=1))
        l_sc[...] = alpha * l_sc[...] + jnp.sum(p, axis=-1, keepdims=True)
        acc_sc[...] = jnp.concatenate([alpha] * (d // LANES), axis=1) * acc_sc[...] + pv_fn(p.astype(BF16))
        m_sc[...] = m_new

    qbd = qbd_sc[...]
    off = carry_sc[...]
    scores = []
    for pi in range(pg):
        loc = _dot_f32_lhs(lf_refs[pi][...], u_ref[...])
        cpage = loc + off
        off = off + loc[:, LANES - 1:LANES]
        kt = k_refs[pi][...].astype(BF16)
        scores.append(_dot(qbd, kt) - jnp.concatenate([cpage] * nq, axis=0))
    carry_sc[...] = off

    def pv_pages(p):
        acc = None
        for pi in range(pg):
            part = _dot(p[:, pi * LANES:(pi + 1) * LANES], v_refs[pi][...].astype(BF16), NT)
            acc = part if acc is None else acc + part
        return acc

    update(jnp.concatenate(scores, axis=1), pv_pages)

    @pl.when(g == pl.num_programs(1) - 1)
    def _():
        lfo_ref[...] = _log_sigmoid(f_ref[:, 0:nh] + bfr_ref[...])
        lft = _log_sigmoid(ft_ref[...] + bfc_ref[...])
        lane = lax.broadcasted_iota(jnp.int32, (nh, LANES), 1)
        cnew = jnp.zeros((nh, LANES), F32)
        run = carry_sc[...]
        for t in range(nq):
            run = run + lft[:, t:t + 1]
            cnew = jnp.where(lane == t, run, cnew)
        knp_sc[...] = jnp.zeros(knp_sc.shape, F32)
        knp_sc[0:nq, :] = kn_ref[...]
        vnp_sc[...] = jnp.zeros(vnp_sc.shape, F32)
        vnp_sc[0:nq, :] = vn_ref[...]
        s = _dot(qbd_sc[...], knp_sc[...].astype(BF16), NT) - jnp.concatenate([cnew] * nq, axis=0)
        key = lax.broadcasted_iota(jnp.int32, (rows, LANES), 1)
        ridx = lax.broadcasted_iota(jnp.int32, (rows, LANES), 0)
        qrow = jnp.zeros((rows, LANES), jnp.int32)
        for qi in range(1, nq):
            qrow = qrow + (ridx >= qi * nh).astype(jnp.int32)
        s = jnp.where(key <= qrow, s, -jnp.inf)
        vnb = vnp_sc[...].astype(BF16)
        update(s, lambda p: _dot(p, vnb))
        out = acc_sc[...] / l_sc[:, 0:1]
        hm = hm_ref[...]
        for qi in range(nq):
            o_ref[qi:qi + 1, :] = jnp.sum(out[qi * nh:(qi + 1) * nh, :] * hm, axis=0,
                                          keepdims=True).astype(o_ref.dtype)


def _attn_sample(page_table, proj3, ft3, bf_row, bf_col, utri, hmask, kt_cache, vt_cache, lf_cache,
                 *, d, pg=16):
    nseq, nq, _ = proj3.shape
    npages = page_table.shape[1]
    nh = hmask.shape[0]
    rows = nq * nh
    scale = HEAD_DIM ** -0.5
    page = kt_cache.shape[-1]
    fblk = (3 * d) // LANES

    def pmap(pi):
        return lambda b, g, pt: (pt[b, g * pg + pi], 0, 0, 0)

    cmap = lambda b, g, pt: (0, 0)
    in_specs = [pl.BlockSpec((None, nq, d), lambda b, g, pt: (b, 0, 0)),
                pl.BlockSpec((None, nq, d), lambda b, g, pt: (b, 0, 1)),
                pl.BlockSpec((None, nq, d), lambda b, g, pt: (b, 0, 2)),
                pl.BlockSpec((None, nq, LANES), lambda b, g, pt: (b, 0, fblk)),
                pl.BlockSpec((None, nh, nq), lambda b, g, pt: (b, 0, 0)),
                pl.BlockSpec((1, nh), cmap), pl.BlockSpec((nh, 1), cmap),
                pl.BlockSpec((page, page), cmap), pl.BlockSpec((nh, d), cmap)]
    in_specs += [pl.BlockSpec((None, None, d, page), pmap(pi)) for pi in range(pg)]
    in_specs += [pl.BlockSpec((None, None, d, page), pmap(pi)) for pi in range(pg)]
    in_specs += [pl.BlockSpec((None, None, nh, page), pmap(pi)) for pi in range(pg)]
    grid_spec = pltpu.PrefetchScalarGridSpec(
        num_scalar_prefetch=1,
        grid=(nseq, npages // pg),
        in_specs=in_specs,
        out_specs=[pl.BlockSpec((None, nq, d), lambda b, g, pt: (b, 0, 0)),
                   pl.BlockSpec((None, nq, nh), lambda b, g, pt: (b, 0, 0))],
        scratch_shapes=[pltpu.VMEM((rows, d), BF16), pltpu.VMEM((rows, LANES), F32),
                        pltpu.VMEM((rows, LANES), F32),
                        pltpu.VMEM((rows, d), F32), pltpu.VMEM((nh, 1), F32),
                        pltpu.VMEM((LANES, d), F32), pltpu.VMEM((LANES, d), F32)])
    kern = functools.partial(_attn_sample_kernel, pg=pg, nq=nq, nh=nh, scale=scale)
    return pl.pallas_call(
        kern,
        grid_spec=grid_spec,
        out_shape=[jax.ShapeDtypeStruct((nseq, nq, d), F32), jax.ShapeDtypeStruct((nseq, nq, nh), F32)],
        compiler_params=_cparams(("parallel", "arbitrary")),
        name="fox_attn_sample",
    )(page_table, proj3, proj3, proj3, proj3, ft3, bf_row, bf_col, utri, hmask,
      *([kt_cache] * pg), *([vt_cache] * pg), *([lf_cache] * pg))


def _tri_upper(n):
    r = lax.broadcasted_iota(jnp.int32, (n, n), 0)
    c = lax.broadcasted_iota(jnp.int32, (n, n), 1)
    return (r <= c).astype(BF16)


def kernel(x_prompt, x_sample, cache_k, cache_v, cache_logf, page_table, state_ssm, state_conv,
           norm_mix_w, norm_mlp_w, norm_out_w, fox_w_in, fox_b_f, fox_w_out,
           ssd_w_in, ssd_conv_w, ssd_conv_b, ssd_dt_bias, ssd_a_log, ssd_d, ssd_norm_w, ssd_w_out,
           mlp_w_up, mlp_w_down):
    bp, lp, d = x_prompt.shape
    bs, ls, _ = x_sample.shape
    nh = fox_b_f.shape[-1]
    nsh = ssd_dt_bias.shape[-1]
    di = nsh * HEAD_DIM
    nzx = ssd_w_in.shape[-1] - nsh
    page = cache_k.shape[2]

    fox_wt = jnp.transpose(fox_w_in[0]).astype(BF16)
    wq, wk, wv, wf = fox_wt[0:d], fox_wt[d:2 * d], fox_wt[2 * d:3 * d], fox_wt[3 * d:]
    npad = (-fox_wt.shape[0]) % LANES
    fox_wt_pad = jnp.concatenate([fox_wt, jnp.zeros((npad, d), BF16)], axis=0)
    bf_row = fox_b_f[0].reshape(1, nh)
    bf_col = fox_b_f[0].reshape(nh, 1)
    fox_wo = fox_w_out[0].astype(BF16)
    ssd_wt = jnp.transpose(ssd_w_in[0]).astype(BF16)
    wzx, wdt = ssd_wt[0:nzx], ssd_wt[nzx:]
    wdtp = jnp.concatenate([wdt, jnp.zeros((LANES - nsh, d), BF16)], axis=0)
    dtb_row = jnp.concatenate([ssd_dt_bias[0], jnp.zeros((LANES - nsh,), F32)]).reshape(1, LANES)
    dtb_col = ssd_dt_bias[0].reshape(nsh, 1)
    a_neg = -jnp.exp(ssd_a_log[0])
    a_row = jnp.concatenate([a_neg, jnp.zeros((LANES - nsh,), F32)]).reshape(1, LANES)
    a_col = a_neg.reshape(nsh, 1)
    d_row = jnp.repeat(ssd_d[0], HEAD_DIM).reshape(1, di)
    ssd_nw = ssd_norm_w[0].reshape(1, di)
    ssd_wo = ssd_w_out[0].astype(BF16)
    conv_w = ssd_conv_w[0]
    conv_b = ssd_conv_b[0].reshape(1, -1)
    wup = mlp_w_up.astype(BF16)
    wdn = mlp_w_down.astype(BF16)
    nmix = norm_mix_w.reshape(-1, 1, d)
    nmlp = norm_mlp_w.reshape(-1, 1, d)
    nout = norm_out_w.reshape(1, d)
    utri = _tri_upper(LANES)
    ltri = jnp.transpose(utri)
    utri_c = _tri_upper(256)
    hmask = (lax.broadcasted_iota(jnp.int32, (nh, d), 1) // HEAD_DIM
             == lax.broadcasted_iota(jnp.int32, (nh, d), 0)).astype(F32)

    def ssd_layer(x, batch, seq, h0, tail):
        gy, hlast, tx, tbc = _ssd_mixer(x.reshape(batch, seq, d), nmix[1], wzx, wdtp, wdt, dtb_row, dtb_col,
                                        h0, tail, conv_w, conv_b, a_row, a_col, d_row, ssd_nw, ltri, utri)
        x = _linear_residual(gy.reshape(batch * seq, di), ssd_wo, x)
        y = _mlp_block(x, nmlp[1], wup[1], wdn[1], nout, final_norm=True)
        new_conv = jnp.concatenate([tx[:, 5:], tbc[:, 5:]], axis=-1).reshape(batch, 1, 3, nzx - di)
        return y, hlast.reshape(batch, 1, nsh, HEAD_DIM, SSD_STATE), new_conv

    tp = bp * lp
    xp = x_prompt.reshape(tp, d)
    q, kt, vt, ktb, vtb, lft = _fox_in_prompt(xp, nmix[0], wq, wk, wv, wf, bf_col, batch=bp, seq=lp)
    ct = _cumsum_lanes(lft, utri_c)
    o = _attn_prompt(q, ktb, vtb, ct.reshape(bp, nh // 2, 2, lp), batch=bp, seq=lp)
    xp = _linear_residual(o, fox_wo, xp)
    xp = _mlp_block(xp, nmlp[0], wup[0], wdn[0], nout, final_norm=False)
    h0_p = jnp.zeros((bp, nsh * HEAD_DIM, SSD_STATE), state_ssm.dtype)
    tail_p = jnp.zeros((bp, 8, nzx - di), F32)
    y_p, ssm_p, conv_p = ssd_layer(xp, bp, lp, h0_p, tail_p)
    y_prompt = y_p.reshape(bp, lp, d)
    new_k_prompt = jnp.transpose(kt.reshape(bp, 1, nh, HEAD_DIM, lp), (0, 1, 4, 2, 3))
    new_v_prompt = jnp.transpose(vt.reshape(bp, 1, nh, HEAD_DIM, lp), (0, 1, 4, 2, 3))
    new_logf_prompt = jnp.transpose(lft.reshape(bp, 1, nh, lp), (0, 1, 3, 2))

    ts = bs * ls
    xs = x_sample.reshape(ts, d)
    proj = _norm_linear(xs, nmix[0], fox_wt_pad)
    proj3 = proj.reshape(bs, ls, -1)
    ft3 = jnp.transpose(proj3[:, :, 3 * d:3 * d + nh], (0, 2, 1))
    kt_cache = jnp.transpose(cache_k, (0, 1, 3, 4, 2)).reshape(cache_k.shape[0], cache_k.shape[1], d, page)
    vt_cache = jnp.transpose(cache_v, (0, 1, 3, 4, 2)).reshape(cache_v.shape[0], cache_v.shape[1], d, page)
    lf_cache = jnp.transpose(cache_logf, (0, 1, 3, 2))
    o_s, lf_s = _attn_sample(page_table, proj3, ft3, bf_row, bf_col, utri, hmask,
                             kt_cache, vt_cache, lf_cache, d=d)
    xs = _linear_residual(o_s.reshape(ts, d), fox_wo, xs)
    xs = _mlp_block(xs, nmlp[0], wup[0], wdn[0], nout, final_norm=False)
    h0_s = state_ssm[:, 0].reshape(bs, nsh * HEAD_DIM, SSD_STATE)
    tail_s = jnp.concatenate([jnp.zeros((bs, 5, nzx - di), F32), state_conv[:, 0]], axis=1)
    y_s, ssm_s, conv_s = ssd_layer(xs, bs, ls, h0_s, tail_s)
    y_sample = y_s.reshape(bs, ls, d)
    new_k_sample = proj3[:, :, d:2 * d].reshape(bs, 1, ls, nh, HEAD_DIM)
    new_v_sample = proj3[:, :, 2 * d:3 * d].reshape(bs, 1, ls, nh, HEAD_DIM)
    new_logf_sample = lf_s.reshape(bs, 1, ls, nh)

    return (y_prompt, y_sample, new_k_prompt, new_v_prompt, new_logf_prompt, ssm_p, conv_p,
            new_k_sample, new_v_sample, new_logf_sample, ssm_s, conv_s)
```

```python
import functools

import jax
import jax.numpy as jnp
from jax import lax
from jax.experimental import pallas as pl
from jax.experimental.pallas import tpu as pltpu

F32 = jnp.float32
BF16 = jnp.bfloat16
EPS = 1e-5
LANES = 128
HEAD_DIM = 64
SSD_STATE = 128
SSD_GROUP_W = 256
VMEM_LIMIT = 52 * 1024 * 1024

NT = (((1,), (1,)), ((), ()))
NN = (((1,), (0,)), ((), ()))
TN = (((0,), (0,)), ((), ()))


def _cparams(sem):
    return pltpu.CompilerParams(dimension_semantics=sem, vmem_limit_bytes=VMEM_LIMIT)


def _dot(a, b, dims=NN):
    return lax.dot_general(a, b, dims, preferred_element_type=F32)


def _split3(x):
    hi = x.astype(BF16)
    r = x - hi.astype(F32)
    mid = r.astype(BF16)
    lo = (r - mid.astype(F32)).astype(BF16)
    return hi, mid, lo


def _dot_f32_lhs(x, sel, dims=NN):
    hi, mid, lo = _split3(x)
    return _dot(hi, sel, dims) + _dot(mid, sel, dims) + _dot(lo, sel, dims)


def _dot_f32_rhs(sel, x, dims=NN):
    hi, mid, lo = _split3(x)
    return _dot(sel, hi, dims) + _dot(sel, mid, dims) + _dot(sel, lo, dims)


def _rmsnorm(x, w):
    ms = jnp.mean(x * x, axis=-1, keepdims=True)
    return x * lax.rsqrt(ms + EPS) * w


def _softplus(x):
    return jnp.maximum(x, 0.0) + jnp.log1p(jnp.exp(-jnp.abs(x)))


def _log_sigmoid(x):
    return -_softplus(-x)


def _silu(x):
    hx = 0.5 * x
    return hx + hx * jnp.tanh(hx)


def _fox_in_prompt_kernel(x_ref, nw_ref, wq_ref, wk_ref, wv_ref, wf_ref, bf_ref,
                          q_ref, kt_ref, vt_ref, ktb_ref, vtb_ref, lft_ref, *, scale, nchunk):
    xn = _rmsnorm(x_ref[...], nw_ref[...]).astype(BF16)
    d = xn.shape[1]
    cw = d // nchunk
    for c in range(nchunk):
        sl = slice(c * cw, (c + 1) * cw)
        q = _dot(xn, wq_ref[sl, :], NT)
        q_ref[:, sl] = (q * scale).astype(BF16)
        kt = _dot(wk_ref[sl, :], xn, NT)
        kt_ref[0, sl, :] = kt
        ktb_ref[0, sl, :] = kt.astype(BF16)
        vt = _dot(wv_ref[sl, :], xn, NT)
        vt_ref[0, sl, :] = vt
        vtb_ref[0, sl, :] = vt.astype(BF16)
    ft = _dot(wf_ref[...], xn, NT)
    lft_ref[0] = _log_sigmoid(ft + bf_ref[...])


def _fox_in_prompt(x, nw, wq, wk, wv, wf, bf_col, *, batch, seq, tm=512):
    t, d = x.shape
    nh = wf.shape[0]
    nt = seq // tm
    scale = HEAD_DIM ** -0.5
    row = lambda i: (i, 0)
    full = lambda i: (0, 0)
    tr = lambda i: (i // nt, 0, i % nt)
    kern = functools.partial(_fox_in_prompt_kernel, scale=scale, nchunk=4)
    return pl.pallas_call(
        kern,
        grid=(t // tm,),
        in_specs=[pl.BlockSpec((tm, d), row), pl.BlockSpec((1, d), full),
                  pl.BlockSpec((d, d), full), pl.BlockSpec((d, d), full), pl.BlockSpec((d, d), full),
                  pl.BlockSpec((nh, d), full), pl.BlockSpec((nh, 1), full)],
        out_specs=[pl.BlockSpec((tm, d), row),
                   pl.BlockSpec((1, d, tm), tr), pl.BlockSpec((1, d, tm), tr),
                   pl.BlockSpec((1, d, tm), tr), pl.BlockSpec((1, d, tm), tr),
                   pl.BlockSpec((1, nh, tm), tr)],
        out_shape=[jax.ShapeDtypeStruct((t, d), BF16),
                   jax.ShapeDtypeStruct((batch, d, seq), F32), jax.ShapeDtypeStruct((batch, d, seq), F32),
                   jax.ShapeDtypeStruct((batch, d, seq), BF16), jax.ShapeDtypeStruct((batch, d, seq), BF16),
                   jax.ShapeDtypeStruct((batch, nh, seq), F32)],
        compiler_params=_cparams(("parallel",)),
        name="fox_in_prompt",
    )(x, nw, wq, wk, wv, wf, bf_col)


def _cumsum_kernel(lf_ref, u_ref, c_ref, *, chunk):
    nh, seq = lf_ref.shape[1], lf_ref.shape[2]
    carry = jnp.zeros((nh, 1), F32)
    for j in range(seq // chunk):
        sl = slice(j * chunk, (j + 1) * chunk)
        c = _dot_f32_lhs(lf_ref[0, :, sl], u_ref[...]) + carry
        c_ref[0, :, sl] = c
        carry = c[:, chunk - 1:chunk]


def _cumsum_lanes(lft, utri):
    b, nh, seq = lft.shape
    chunk = utri.shape[0]
    return pl.pallas_call(
        functools.partial(_cumsum_kernel, chunk=chunk),
        grid=(b,),
        in_specs=[pl.BlockSpec((1, nh, seq), lambda i: (i, 0, 0)),
                  pl.BlockSpec((chunk, chunk), lambda i: (0, 0))],
        out_specs=pl.BlockSpec((1, nh, seq), lambda i: (i, 0, 0)),
        out_shape=jax.ShapeDtypeStruct((b, nh, seq), F32),
        compiler_params=_cparams(("parallel",)),
        name="fox_cumsum",
    )(lft, utri)


N_SPLIT = 3


def _attn_prompt_kernel(q_ref, kt_ref, vt_ref, c_ref, o_ref, m_sc, acc_sc, *, blk, wide, dsplit):
    i = pl.program_id(2)
    q2 = q_ref[...].astype(F32)
    lane = lax.broadcasted_iota(jnp.int32, (blk, LANES), 1)
    lo = lane < HEAD_DIM
    qe = [jnp.where(lo, q2, jnp.where(lane < HEAD_DIM + N_SPLIT, 1.0, 0.0)).astype(BF16),
          jnp.where(lo, jnp.where(lane < N_SPLIT, 1.0, 0.0), q2).astype(BF16)]
    m_sc[...] = jnp.full(m_sc.shape, -jnp.inf, F32)
    acc_sc[...] = jnp.zeros(acc_sc.shape, F32)
    hb = blk // dsplit
    consts = {}
    for wk in sorted({blk, wide * blk} | {(t + 1) * hb for t in range(dsplit)}):
        rowi = lax.broadcasted_iota(jnp.int32, (16, wk), 0)
        pad = jnp.zeros((HEAD_DIM - 16, wk), BF16)
        ones_blk = jnp.concatenate([jnp.where(rowi == 0, 1.0, 0.0).astype(BF16), pad], axis=0)
        consts[wk] = (rowi, pad, ones_blk)

    def step(off, wk, r0=0, nr=blk, shift=None):
        rowi, pad, ones_blk = consts[wk]
        nrep = wk // LANES
        rows = slice(r0, r0 + nr)
        ck = c_ref[0, 0, :, pl.ds(off, wk)]
        if shift is not None:
            r = lax.broadcasted_iota(jnp.int32, (nr, wk), 0)
            cidx = lax.broadcasted_iota(jnp.int32, (nr, wk), 1)
            vis = cidx <= r + shift
        scores, vtes = [], []
        for h in range(2):
            hs = slice(h * HEAD_DIM, (h + 1) * HEAD_DIM)
            hi, mid, low = _split3(-ck[h:h + 1, :])
            b16 = jnp.where(rowi == 0, hi.astype(F32),
                            jnp.where(rowi == 1, mid.astype(F32),
                                      jnp.where(rowi == 2, low.astype(F32), 0.0))).astype(BF16)
            bias_blk = jnp.concatenate([b16, pad], axis=0)
            kth = kt_ref[0, hs, pl.ds(off, wk)]
            vth = vt_ref[0, hs, pl.ds(off, wk)]
            if h == 0:
                kte = jnp.concatenate([kth, bias_blk], axis=0)
                vtes.append(jnp.concatenate([vth, ones_blk], axis=0))
            else:
                kte = jnp.concatenate([bias_blk, kth], axis=0)
                vtes.append(jnp.concatenate([ones_blk, vth], axis=0))
            s = _dot(qe[h][rows], kte)
            if shift is not None:
                s = jnp.where(vis, s, -jnp.inf)
            scores.append(s)
        probs, alphas = [], []
        for h in range(2):
            m_prev = m_sc[h, rows, :]
            m_new = jnp.maximum(m_prev, jnp.max(scores[h], axis=-1, keepdims=True))
            alphas.append(jnp.exp(m_prev - m_new))
            probs.append(jnp.exp(scores[h] - jnp.concatenate([m_new] * nrep, axis=1)).astype(BF16))
            m_sc[h, rows, :] = m_new
        for h in range(2):
            pv = _dot(probs[h], vtes[h], NT)
            acc_sc[h, rows, :] = alphas[h] * acc_sc[h, rows, :] + pv

    def body(j, carry):
        step(pl.multiple_of(j * (wide * blk), wide * blk), wide * blk)
        return carry

    nwide = i // wide
    lax.fori_loop(0, nwide, body, 0)
    for extra in range(wide - 1):
        @pl.when(nwide * wide + extra < i)
        def _():
            step(pl.multiple_of((nwide * wide + extra) * blk, blk), blk)
    for t in range(dsplit):
        step(pl.multiple_of(i * blk, blk), (t + 1) * hb, r0=t * hb, nr=hb, shift=t * hb)
    a0 = acc_sc[0]
    a1 = acc_sc[1]
    out0 = a0 / a0[:, HEAD_DIM:HEAD_DIM + 1]
    out1 = a1 / a1[:, 0:1]
    o_ref[...] = jnp.where(lo, out0, out1).astype(o_ref.dtype)


def _attn_prompt(q, ktb, vtb, c4, *, batch, seq, blk=512, wide=2, dsplit=1):
    t, d = q.shape
    npair = d // LANES
    nq = seq // blk
    qmap = lambda b, h, i: (b * nq + i, h)
    kmap = lambda b, h, i: (b, h, 0)
    return pl.pallas_call(
        functools.partial(_attn_prompt_kernel, blk=blk, wide=wide, dsplit=dsplit),
        grid=(batch, npair, nq),
        in_specs=[pl.BlockSpec((blk, LANES), qmap),
                  pl.BlockSpec((1, LANES, seq), kmap), pl.BlockSpec((1, LANES, seq), kmap),
                  pl.BlockSpec((1, 1, 2, seq), lambda b, h, i: (b, h, 0, 0))],
        out_specs=pl.BlockSpec((blk, LANES), qmap),
        out_shape=jax.ShapeDtypeStruct((t, d), BF16),
        scratch_shapes=[pltpu.VMEM((2, blk, LANES), F32), pltpu.VMEM((2, blk, LANES), F32)],
        compiler_params=_cparams(("parallel", "parallel", "arbitrary")),
        name="fox_attn_prompt",
    )(q, ktb, vtb, c4)


def _mlp_kernel(x_ref, a_ref, wo_ref, nw_ref, wup_ref, wdn_ref, fw_ref, o_ref, xn_sc, acc_sc, *, final_norm):
    f = pl.program_id(1)

    @pl.when(f == 0)
    def _():
        x = x_ref[...] + _dot(a_ref[...].astype(BF16), wo_ref[...])
        xn_sc[...] = _rmsnorm(x, nw_ref[...]).astype(BF16)
        acc_sc[...] = x

    h = jnp.maximum(_dot(xn_sc[...], wup_ref[...]), 0.0)
    acc_sc[...] += _dot((h * h).astype(BF16), wdn_ref[...])

    @pl.when(f == pl.num_programs(1) - 1)
    def _():
        y = acc_sc[...]
        if final_norm:
            y = _rmsnorm(y, fw_ref[...])
        o_ref[...] = y


def _mlp_block(x, a, wo, nw, wup, wdn, fw, *, final_norm, tm=1024, tf=1024):
    t, d = x.shape
    k = a.shape[1]
    dff = wup.shape[1]
    tm = min(tm, t)
    return pl.pallas_call(
        functools.partial(_mlp_kernel, final_norm=final_norm),
        grid=(t // tm, dff // tf),
        in_specs=[pl.BlockSpec((tm, d), lambda i, f: (i, 0)), pl.BlockSpec((tm, k), lambda i, f: (i, 0)),
                  pl.BlockSpec((k, d), lambda i, f: (0, 0)), pl.BlockSpec((1, d), lambda i, f: (0, 0)),
                  pl.BlockSpec((d, tf), lambda i, f: (0, f)), pl.BlockSpec((tf, d), lambda i, f: (f, 0)),
                  pl.BlockSpec((1, d), lambda i, f: (0, 0))],
        out_specs=pl.BlockSpec((tm, d), lambda i, f: (i, 0)),
        out_shape=jax.ShapeDtypeStruct((t, d), F32),
        scratch_shapes=[pltpu.VMEM((tm, d), BF16), pltpu.VMEM((tm, d), F32)],
        compiler_params=_cparams(("parallel", "arbitrary")),
        name="mlp_block",
    )(x, a, wo, nw, wup, wdn, fw)


def _norm_linear_kernel(x_ref, nw_ref, wt_ref, o_ref):
    xn = _rmsnorm(x_ref[...], nw_ref[...]).astype(BF16)
    o_ref[...] = _dot(xn, wt_ref[...], NT)


def _norm_linear(x, nw, wt, *, tm=128):
    t, d = x.shape
    n = wt.shape[0]
    return pl.pallas_call(
        _norm_linear_kernel,
        grid=(t // tm,),
        in_specs=[pl.BlockSpec((tm, d), lambda i: (i, 0)), pl.BlockSpec((1, d), lambda i: (0, 0)),
                  pl.BlockSpec((n, d), lambda i: (0, 0))],
        out_specs=pl.BlockSpec((tm, n), lambda i: (i, 0)),
        out_shape=jax.ShapeDtypeStruct((t, n), F32),
        compiler_params=_cparams(("parallel",)),
        name="norm_linear",
    )(x, nw, wt)


def _ssd_mixer_kernel(*refs, q, valid, nheads, ncol, pipe, preproj):
    nlead = 5 if preproj else 7
    (h0_ref, tx_ref, tbc_ref,
     cwx_ref, cwbc_ref, cbx_ref, cbbc_ref, arow_ref, acol_ref, drow_ref, nw_ref,
     ltri_ref, utri_ref,
     gy_ref, hout_ref, txo_ref, tbco_ref,
     h_sc, xpad_sc, bcpad_sc, z_sc, dtn_sc, dtt_sc, x_sc,
     sz_sc, sx_sc, sbc_sc, sdtn_sc, sdtt_sc) = refs[nlead:]
    c = pl.program_id(1)
    di = z_sc.shape[1]
    ngroups = di // SSD_GROUP_W
    first_scan = 1 if pipe else 0

    @pl.when(c == 0)
    def _():
        if pipe:
            for ref in (sz_sc, sx_sc, sbc_sc, sdtn_sc, sdtt_sc):
                ref[...] = jnp.zeros(ref.shape, F32)
        xpad_sc[0:8, :] = tx_ref[...]
        bcpad_sc[0:8, :] = tbc_ref[...]

    @pl.when(c <= first_scan)
    def _():
        h_sc[...] = h0_ref[...]

    if pipe:
        z_sc[...] = sz_sc[...]
        xpad_sc[8:8 + q, :] = sx_sc[...]
        bcpad_sc[8:8 + q, :] = sbc_sc[...]
        dtn_sc[...] = sdtn_sc[...]
        dtt_sc[...] = sdtt_sc[...]
        pz, px, pbc, pdtn, pdtt, prow = sz_sc, sx_sc, sbc_sc, sdtn_sc, sdtt_sc, 0
    else:
        pz, px, pbc, pdtn, pdtt, prow = z_sc, xpad_sc, bcpad_sc, dtn_sc, dtt_sc, 8

    if preproj:
        zin_ref, xin_ref, bcin_ref, dtnin_ref, dttin_ref = refs[:nlead]
        assert not pipe
        for dst, src, r0 in ((z_sc, zin_ref, 0), (xpad_sc, xin_ref, 8), (bcpad_sc, bcin_ref, 8),
                             (dtn_sc, dtnin_ref, 0)):
            if valid != q:
                dst[r0:r0 + q, :] = jnp.zeros((q, dst.shape[1]), F32)
            dst[r0:r0 + valid, :] = src[...]
        if valid != q:
            dtt_sc[...] = jnp.zeros(dtt_sc.shape, F32)
        dtt_sc[:, 0:valid] = dttin_ref[...]
    else:
        x_ref, nwx_ref, wzx_ref, wdtp_ref, wdt_ref, brow_ref, bcol_ref = refs[:nlead]
        if valid == q:
            x = x_ref[...]
        else:
            x_sc[...] = jnp.zeros(x_sc.shape, F32)
            x_sc[0:valid, :] = x_ref[...]
            x = x_sc[...]
        xn = _rmsnorm(x, nwx_ref[...]).astype(BF16)
        dtn_p = _softplus(_dot(xn, wdtp_ref[...], NT) + brow_ref[...])
        dtt_p = _softplus(_dot(wdt_ref[...], xn, NT) + bcol_ref[...])
        if valid != q:
            dtn_p = jnp.where(lax.broadcasted_iota(jnp.int32, dtn_p.shape, 0) < valid, dtn_p, 0.0)
            dtt_p = jnp.where(lax.broadcasted_iota(jnp.int32, dtt_p.shape, 1) < valid, dtt_p, 0.0)
        pdtn[...] = dtn_p
        pdtt[...] = dtt_p
        cw = (3 * di) // ncol

        def project(k):
            piece = _dot(xn, wzx_ref[k * cw:(k + 1) * cw, :], NT)
            tgt, off = divmod(k * cw, di)
            if tgt == 0:
                pz[:, off:off + cw] = piece
            elif tgt == 1:
                px[prow:prow + q, off:off + cw] = piece
            else:
                pbc[prow:prow + q, off:off + cw] = piece

        if not pipe:
            for k in range(ncol):
                project(k)

    def conv(pad_sc, w_ref, b_ref, sl):
        w = w_ref[:, sl]
        acc = b_ref[:, sl] + pad_sc[5:5 + q, sl] * w[0:1, :]
        acc = acc + pad_sc[6:6 + q, sl] * w[1:2, :]
        acc = acc + pad_sc[7:7 + q, sl] * w[2:3, :]
        acc = acc + pad_sc[8:8 + q, sl] * w[3:4, :]
        return _silu(acc)

    dtn = dtn_sc[...]
    dtt = dtt_sc[...]
    a_nat = dtn * arow_ref[...]
    a_t = dtt * acol_ref[...]
    cum_nat = _dot_f32_rhs(ltri_ref[...], a_nat)
    cum_t = _dot_f32_lhs(a_t, utri_ref[...])
    ri = lax.broadcasted_iota(jnp.int32, (q, q), 0)
    ci = lax.broadcasted_iota(jnp.int32, (q, q), 1)
    tri = ci <= ri
    lo = lax.broadcasted_iota(jnp.int32, (q, LANES), 1) < HEAD_DIM
    rlo = lax.broadcasted_iota(jnp.int32, (LANES, SSD_STATE), 0) < HEAD_DIM
    nb = di // 2
    clast_row = cum_nat[q - 1:q, :]
    e_nat = jnp.exp(cum_nat)
    w_nat = jnp.exp(clast_row - cum_nat) * dtn
    dl_row = jnp.exp(clast_row)

    for g in range(ngroups):
        if pipe:
            for k in range(g * ncol // ngroups, (g + 1) * ncol // ngroups):
                project(k)
        bg = conv(bcpad_sc, cwbc_ref, cbbc_ref, slice(g * SSD_STATE, (g + 1) * SSD_STATE))
        cg = conv(bcpad_sc, cwbc_ref, cbbc_ref, slice(nb + g * SSD_STATE, nb + (g + 1) * SSD_STATE))
        bb = bg.astype(BF16)
        cb16 = cg.astype(BF16)
        cb = _dot(cb16, bb, NT)
        gated = []
        ssq = jnp.zeros((q, 1), F32)
        for pr in range(2):
            p = 2 * g + pr
            sl = slice(p * LANES, (p + 1) * LANES)
            xp = conv(xpad_sc, cwx_ref, cbx_ref, sl)
            xpb = xp.astype(BF16)
            yi, ee, we, dl = [], [], [], []
            for r in range(2):
                h = 2 * p + r
                colb = jnp.broadcast_to(cum_nat[:, h:h + 1], (q, q))
                rowb = cum_t[h:h + 1, :]
                dec = jnp.exp(jnp.where(tri, colb - rowb, -jnp.inf))
                w = (cb * dec * dtt[h:h + 1, :]).astype(BF16)
                yi.append(_dot(w, xpb))
                ee.append(jnp.broadcast_to(e_nat[:, h:h + 1], (q, LANES)))
                we.append(jnp.broadcast_to(w_nat[:, h:h + 1], (q, LANES)))
                dl.append(jnp.broadcast_to(dl_row[:, h:h + 1], (LANES, SSD_STATE)))
            hp = h_sc[sl, :]
            y_inter = _dot(cb16, hp.astype(BF16), NT) * jnp.where(lo, ee[0], ee[1])
            xw = (xp * jnp.where(lo, we[0], we[1])).astype(BF16)
            h_sc[sl, :] = jnp.where(rlo, dl[0], dl[1]) * hp + _dot(xw, bb, TN)
            y = jnp.where(lo, yi[0], yi[1]) + y_inter + xp * drow_ref[:, sl]
            gt = y * _silu(z_sc[:, sl])
            ssq = ssq + jnp.sum(gt * gt, axis=-1, keepdims=True)
            gated.append(gt)
        rs = lax.rsqrt(ssq / SSD_GROUP_W + EPS)
        for pr in range(2):
            sl = slice((2 * g + pr) * LANES, (2 * g + pr + 1) * LANES)
            gy_ref[:, sl] = (gated[pr] * rs * nw_ref[:, sl])[0:valid].astype(gy_ref.dtype)

    @pl.when(c == pl.num_programs(1) - 1)
    def _():
        hout_ref[...] = h_sc[...]
        txo_ref[...] = xpad_sc[valid:valid + 8, :]
        tbco_ref[...] = bcpad_sc[valid:valid + 8, :]

    @pl.when(c >= first_scan)
    def _():
        xpad_sc[0:8, :] = xpad_sc[q:q + 8, :]
        bcpad_sc[0:8, :] = bcpad_sc[q:q + 8, :]


def _ssd_proj_kernel(x_ref, nw_ref, w_ref, wdtp_ref, wdt_ref, brow_ref, bcol_ref, zx_ref, dtn_ref, dtt_ref):
    xn = _rmsnorm(x_ref[...], nw_ref[...]).astype(BF16)
    dtn_ref[...] = _softplus(_dot(xn, wdtp_ref[...], NT) + brow_ref[...])
    dtt_ref[...] = _softplus(_dot(wdt_ref[...], xn, NT) + bcol_ref[...])
    zx_ref[...] = _dot(xn, w_ref[...], NT)


def _ssd_proj(x, nw, wzx, wdtp, wdt, brow, bcol, *, tn=1024):
    t, d = x.shape
    n = wzx.shape[0]
    nh = wdt.shape[0]
    c0 = lambda j: (0, 0)
    return pl.pallas_call(
        _ssd_proj_kernel,
        grid=(n // tn,),
        in_specs=[pl.BlockSpec((t, d), c0), pl.BlockSpec((1, d), c0), pl.BlockSpec((tn, d), lambda j: (j, 0)),
                  pl.BlockSpec((LANES, d), c0), pl.BlockSpec((nh, d), c0),
                  pl.BlockSpec((1, LANES), c0), pl.BlockSpec((nh, 1), c0)],
        out_specs=[pl.BlockSpec((t, tn), lambda j: (0, j)), pl.BlockSpec((t, LANES), c0),
                   pl.BlockSpec((nh, t), c0)],
        out_shape=[jax.ShapeDtypeStruct((t, n), F32), jax.ShapeDtypeStruct((t, LANES), F32),
                   jax.ShapeDtypeStruct((nh, t), F32)],
        compiler_params=_cparams(("arbitrary",)),
        name="ssd_proj",
    )(x, nw, wzx, wdtp, wdt, brow, bcol)


def _ssd_mixer(x3, nwx, wzx, wdtp, wdt, brow, bcol, h0, tail, cw, cb, arow, acol, drow, nw,
               *, chunk=128, short_chunk=16, ncol=6):
    b, seq, d = x3.shape
    q = chunk if seq >= chunk else short_chunk
    utri = _tri_upper(q)
    ltri = jnp.transpose(utri)
    nheads = wdt.shape[0]
    di = nheads * HEAD_DIM
    valid = min(q, seq)
    nc = max(seq // q, 1)
    pipe = nc > 1
    preproj = not pipe
    nsteps = nc + 1 if pipe else nc
    hrows = nheads * HEAD_DIM
    const2 = lambda k: (lambda i, c: (0, k))
    stage = [pltpu.VMEM((q, di), F32)] * 3 + [pltpu.VMEM((q, LANES), F32), pltpu.VMEM((nheads, q), F32)]
    if not pipe:
        stage = [pltpu.VMEM((8, LANES), F32)] * 5
    if pipe:
        ncol = 3 * (di // SSD_GROUP_W)
    if preproj:
        zx, dtn, dtt = _ssd_proj(x3.reshape(b * seq, d), nwx, wzx, wdtp, wdt, brow, bcol)
        zx3 = zx.reshape(b, seq, 3 * di)
        lead = [zx3, zx3, zx3, dtn.reshape(b, seq, LANES),
                jnp.transpose(dtt.reshape(nheads, b, seq), (1, 0, 2))]
        lead_specs = [pl.BlockSpec((None, valid, di), lambda i, c: (i, c, 0)),
                      pl.BlockSpec((None, valid, di), lambda i, c: (i, c, 1)),
                      pl.BlockSpec((None, valid, di), lambda i, c: (i, c, 2)),
                      pl.BlockSpec((None, valid, LANES), lambda i, c: (i, c, 0)),
                      pl.BlockSpec((None, nheads, valid), lambda i, c: (i, 0, c))]
    else:
        lead = [x3, nwx, wzx, wdtp, wdt, brow, bcol]
        lead_specs = [pl.BlockSpec((None, valid, d), lambda i, c: (i, jnp.minimum(c, nc - 1), 0)),
                      pl.BlockSpec((1, d), const2(0)), pl.BlockSpec((3 * di, d), const2(0)),
                      pl.BlockSpec((LANES, d), const2(0)), pl.BlockSpec((nheads, d), const2(0)),
                      pl.BlockSpec((1, LANES), const2(0)), pl.BlockSpec((nheads, 1), const2(0))]
    kern = functools.partial(_ssd_mixer_kernel, q=q, valid=valid, nheads=nheads, ncol=ncol, pipe=pipe,
                             preproj=preproj)
    return pl.pallas_call(
        kern,
        grid=(b, nsteps),
        in_specs=lead_specs + [
                  pl.BlockSpec((None, hrows, SSD_STATE), lambda i, c: (i, 0, 0)),
                  pl.BlockSpec((None, 8, di), lambda i, c: (i, 0, 0)),
                  pl.BlockSpec((None, 8, di), lambda i, c: (i, 0, 1)),
                  pl.BlockSpec((4, di), const2(0)), pl.BlockSpec((4, di), const2(1)),
                  pl.BlockSpec((1, di), const2(0)), pl.BlockSpec((1, di), const2(1)),
                  pl.BlockSpec((1, LANES), const2(0)), pl.BlockSpec((nheads, 1), const2(0)),
                  pl.BlockSpec((1, di), const2(0)), pl.BlockSpec((1, di), const2(0)),
                  pl.BlockSpec((q, q), const2(0)), pl.BlockSpec((q, q), const2(0))],
        out_specs=[pl.BlockSpec((None, valid, di), lambda i, c: (i, jnp.maximum(c - (nsteps - nc), 0), 0)),
                   pl.BlockSpec((None, hrows, SSD_STATE), lambda i, c: (i, 0, 0)),
                   pl.BlockSpec((None, 8, di), lambda i, c: (i, 0, 0)),
                   pl.BlockSpec((None, 8, di), lambda i, c: (i, 0, 0))],
        out_shape=[jax.ShapeDtypeStruct((b, seq, di), BF16 if valid == q else F32),
                   jax.ShapeDtypeStruct((b, hrows, SSD_STATE), F32),
                   jax.ShapeDtypeStruct((b, 8, di), F32), jax.ShapeDtypeStruct((b, 8, di), F32)],
        scratch_shapes=[pltpu.VMEM((hrows, SSD_STATE), F32),
                        pltpu.VMEM((q + 8, di), F32), pltpu.VMEM((q + 8, di), F32),
                        pltpu.VMEM((q, di), F32), pltpu.VMEM((q, LANES), F32), pltpu.VMEM((nheads, q), F32),
                        pltpu.VMEM((q, d), F32)] + stage,
        compiler_params=_cparams(("parallel", "arbitrary")),
        name="ssd_mixer",
    )(*lead, h0, tail, tail, cw, cw, cb, cb, arow, acol, drow, nw, ltri, utri)


def _attn_sample_kernel(pt_ref, q_ref, kn_ref, vn_ref, f_ref, ft_ref, bfr_ref, bfc_ref, u_ref, hm_ref,
                        *refs, pg, nq, nh, scale):
    k_refs, v_refs, lf_refs = refs[0:pg], refs[pg:2 * pg], refs[2 * pg:3 * pg]
    o_ref, lfo_ref = refs[3 * pg], refs[3 * pg + 1]
    qbd_sc, m_sc, l_sc, acc_sc, carry_sc, knp_sc, vnp_sc = refs[3 * pg + 2:]
    g = pl.program_id(1)
    rows = nq * nh
    d = acc_sc.shape[1]

    @pl.when(g == 0)
    def _():
        qv = q_ref[...] * scale
        hm = hm_ref[...]
        for qi in range(nq):
            qbd_sc[qi * nh:(qi + 1) * nh, :] = (qv[qi:qi + 1, :] * hm).astype(BF16)
        m_sc[...] = jnp.full(m_sc.shape, -jnp.inf, F32)
        l_sc[...] = jnp.zeros(l_sc.shape, F32)
        acc_sc[...] = jnp.zeros(acc_sc.shape, F32)
        carry_sc[...] = jnp.zeros(carry_sc.shape, F32)

    def update(s, pv_fn):
        m_prev = m_sc[...]
        m_new = jnp.maximum(m_prev, jnp.max(s, axis=-1, keepdims=True))
        alpha = jnp.exp(m_prev - m_new)
        p = jnp.exp(s - jnp.concatenate([m_new] * (s.shape[1] // LANES), axis=1))
        l_sc[...] = alpha * l_sc[...] + jnp.sum(p, axis=-1, keepdims=True)
        acc_sc[...] = jnp.concatenate([alpha] * (d // LANES), axis=1) * acc_sc[...] + pv_fn(p.astype(BF16))
        m_sc[...] = m_new

    qbd = qbd_sc[...]
    off = carry_sc[...]
    scores = []
    for pi in range(pg):
        loc = _dot_f32_lhs(lf_refs[pi][...], u_ref[...])
        cpage = loc + off
        off = off + loc[:, LANES - 1:LANES]
        kt = k_refs[pi][...].astype(BF16)
        scores.append(_dot(qbd, kt) - jnp.concatenate([cpage] * nq, axis=0))
    carry_sc[...] = off

    def pv_pages(p):
        acc = None
        for pi in range(pg):
            part = _dot(p[:, pi * LANES:(pi + 1) * LANES], v_refs[pi][...].astype(BF16), NT)
            acc = part if acc is None else acc + part
        return acc

    update(jnp.concatenate(scores, axis=1), pv_pages)

    @pl.when(g == pl.num_programs(1) - 1)
    def _():
        lfo_ref[...] = _log_sigmoid(f_ref[:, 0:nh] + bfr_ref[...])
        lft = _log_sigmoid(ft_ref[...] + bfc_ref[...])
        lane = lax.broadcasted_iota(jnp.int32, (nh, LANES), 1)
        cnew = jnp.zeros((nh, LANES), F32)
        run = carry_sc[...]
        for t in range(nq):
            run = run + lft[:, t:t + 1]
            cnew = jnp.where(lane == t, run, cnew)
        knp_sc[...] = jnp.zeros(knp_sc.shape, F32)
        knp_sc[0:nq, :] = kn_ref[...]
        vnp_sc[...] = jnp.zeros(vnp_sc.shape, F32)
        vnp_sc[0:nq, :] = vn_ref[...]
        s = _dot(qbd_sc[...], knp_sc[...].astype(BF16), NT) - jnp.concatenate([cnew] * nq, axis=0)
        key = lax.broadcasted_iota(jnp.int32, (rows, LANES), 1)
        ridx = lax.broadcasted_iota(jnp.int32, (rows, LANES), 0)
        qrow = jnp.zeros((rows, LANES), jnp.int32)
        for qi in range(1, nq):
            qrow = qrow + (ridx >= qi * nh).astype(jnp.int32)
        s = jnp.where(key <= qrow, s, -jnp.inf)
        vnb = vnp_sc[...].astype(BF16)
        update(s, lambda p: _dot(p, vnb))
        out = acc_sc[...] / l_sc[:, 0:1]
        hm = hm_ref[...]
        for qi in range(nq):
            o_ref[qi:qi + 1, :] = jnp.sum(out[qi * nh:(qi + 1) * nh, :] * hm, axis=0,
                                          keepdims=True).astype(o_ref.dtype)


def _attn_sample(page_table, proj3, ft3, bf_row, bf_col, utri, hmask, kt_cache, vt_cache, lf_cache,
                 *, d, pg=16):
    nseq, nq, _ = proj3.shape
    npages = page_table.shape[1]
    nh = hmask.shape[0]
    rows = nq * nh
    scale = HEAD_DIM ** -0.5
    page = kt_cache.shape[-1]
    fblk = (3 * d) // LANES

    def pmap(pi):
        return lambda b, g, pt: (pt[b, g * pg + pi], 0, 0, 0)

    cmap = lambda b, g, pt: (0, 0)
    in_specs = [pl.BlockSpec((None, nq, d), lambda b, g, pt: (b, 0, 0)),
                pl.BlockSpec((None, nq, d), lambda b, g, pt: (b, 0, 1)),
                pl.BlockSpec((None, nq, d), lambda b, g, pt: (b, 0, 2)),
                pl.BlockSpec((None, nq, LANES), lambda b, g, pt: (b, 0, fblk)),
                pl.BlockSpec((None, nh, nq), lambda b, g, pt: (b, 0, 0)),
                pl.BlockSpec((1, nh), cmap), pl.BlockSpec((nh, 1), cmap),
                pl.BlockSpec((page, page), cmap), pl.BlockSpec((nh, d), cmap)]
    in_specs += [pl.BlockSpec((None, None, d, page), pmap(pi)) for pi in range(pg)]
    in_specs += [pl.BlockSpec((None, None, d, page), pmap(pi)) for pi in range(pg)]
    in_specs += [pl.BlockSpec((None, None, nh, page), pmap(pi)) for pi in range(pg)]
    grid_spec = pltpu.PrefetchScalarGridSpec(
        num_scalar_prefetch=1,
        grid=(nseq, npages // pg),
        in_specs=in_specs,
        out_specs=[pl.BlockSpec((None, nq, d), lambda b, g, pt: (b, 0, 0)),
                   pl.BlockSpec((None, nq, nh), lambda b, g, pt: (b, 0, 0))],
        scratch_shapes=[pltpu.VMEM((rows, d), BF16), pltpu.VMEM((rows, LANES), F32),
                        pltpu.VMEM((rows, LANES), F32),
                        pltpu.VMEM((rows, d), F32), pltpu.VMEM((nh, 1), F32),
                        pltpu.VMEM((LANES, d), F32), pltpu.VMEM((LANES, d), F32)])
    kern = functools.partial(_attn_sample_kernel, pg=pg, nq=nq, nh=nh, scale=scale)
    return pl.pallas_call(
        kern,
        grid_spec=grid_spec,
        out_shape=[jax.ShapeDtypeStruct((nseq, nq, d), F32), jax.ShapeDtypeStruct((nseq, nq, nh), F32)],
        compiler_params=_cparams(("parallel", "arbitrary")),
        name="fox_attn_sample",
    )(page_table, proj3, proj3, proj3, proj3, ft3, bf_row, bf_col, utri, hmask,
      *([kt_cache] * pg), *([vt_cache] * pg), *([lf_cache] * pg))


def _tri_upper(n):
    r = lax.broadcasted_iota(jnp.int32, (n, n), 0)
    c = lax.broadcasted_iota(jnp.int32, (n, n), 1)
    return (r <= c).astype(BF16)


def kernel(x_prompt, x_sample, cache_k, cache_v, cache_logf, page_table, state_ssm, state_conv,
           norm_mix_w, norm_mlp_w, norm_out_w, fox_w_in, fox_b_f, fox_w_out,
           ssd_w_in, ssd_conv_w, ssd_conv_b, ssd_dt_bias, ssd_a_log, ssd_d, ssd_norm_w, ssd_w_out,
           mlp_w_up, mlp_w_down):
    bp, lp, d = x_prompt.shape
    bs, ls, _ = x_sample.shape
    nh = fox_b_f.shape[-1]
    nsh = ssd_dt_bias.shape[-1]
    di = nsh * HEAD_DIM
    nzx = ssd_w_in.shape[-1] - nsh
    page = cache_k.shape[2]

    fox_wt = jnp.transpose(fox_w_in[0]).astype(BF16)
    wq, wk, wv, wf = fox_wt[0:d], fox_wt[d:2 * d], fox_wt[2 * d:3 * d], fox_wt[3 * d:]
    npad = (-fox_wt.shape[0]) % LANES
    fox_wt_pad = jnp.concatenate([fox_wt, jnp.zeros((npad, d), BF16)], axis=0)
    bf_row = fox_b_f[0].reshape(1, nh)
    bf_col = fox_b_f[0].reshape(nh, 1)
    fox_wo = fox_w_out[0].astype(BF16)
    ssd_wt = jnp.transpose(ssd_w_in[0]).astype(BF16)
    wzx, wdt = ssd_wt[0:nzx], ssd_wt[nzx:]
    wdtp = jnp.concatenate([wdt, jnp.zeros((LANES - nsh, d), BF16)], axis=0)
    dtb_row = jnp.concatenate([ssd_dt_bias[0], jnp.zeros((LANES - nsh,), F32)]).reshape(1, LANES)
    dtb_col = ssd_dt_bias[0].reshape(nsh, 1)
    a_neg = -jnp.exp(ssd_a_log[0])
    a_row = jnp.concatenate([a_neg, jnp.zeros((LANES - nsh,), F32)]).reshape(1, LANES)
    a_col = a_neg.reshape(nsh, 1)
    d_row = jnp.repeat(ssd_d[0], HEAD_DIM).reshape(1, di)
    ssd_nw = ssd_norm_w[0].reshape(1, di)
    ssd_wo = ssd_w_out[0].astype(BF16)
    conv_w = ssd_conv_w[0]
    conv_b = ssd_conv_b[0].reshape(1, -1)
    wup = mlp_w_up.astype(BF16)
    wdn = mlp_w_down.astype(BF16)
    nmix = norm_mix_w.reshape(-1, 1, d)
    nmlp = norm_mlp_w.reshape(-1, 1, d)
    nout = norm_out_w.reshape(1, d)
    utri = _tri_upper(LANES)
    utri_c = _tri_upper(256)
    hmask = (lax.broadcasted_iota(jnp.int32, (nh, d), 1) // HEAD_DIM
             == lax.broadcasted_iota(jnp.int32, (nh, d), 0)).astype(F32)

    def ssd_layer(x, batch, seq, h0, tail):
        gy, hlast, tx, tbc = _ssd_mixer(x.reshape(batch, seq, d), nmix[1], wzx, wdtp, wdt, dtb_row, dtb_col,
                                        h0, tail, conv_w, conv_b, a_row, a_col, d_row, ssd_nw)
        y = _mlp_block(x, gy.reshape(batch * seq, di), ssd_wo, nmlp[1], wup[1], wdn[1], nout, final_norm=True)
        new_conv = jnp.concatenate([tx[:, 5:], tbc[:, 5:]], axis=-1).reshape(batch, 1, 3, nzx - di)
        return y, hlast.reshape(batch, 1, nsh, HEAD_DIM, SSD_STATE), new_conv

    tp = bp * lp
    xp = x_prompt.reshape(tp, d)
    q, kt, vt, ktb, vtb, lft = _fox_in_prompt(xp, nmix[0], wq, wk, wv, wf, bf_col, batch=bp, seq=lp)
    ct = _cumsum_lanes(lft, utri_c)
    o = _attn_prompt(q, ktb, vtb, ct.reshape(bp, nh // 2, 2, lp), batch=bp, seq=lp)
    xp = _mlp_block(xp, o, fox_wo, nmlp[0], wup[0], wdn[0], nout, final_norm=False)
    h0_p = jnp.zeros((bp, nsh * HEAD_DIM, SSD_STATE), state_ssm.dtype)
    tail_p = jnp.zeros((bp, 8, nzx - di), F32)
    y_p, ssm_p, conv_p = ssd_layer(xp, bp, lp, h0_p, tail_p)
    y_prompt = y_p.reshape(bp, lp, d)
    new_k_prompt = jnp.transpose(kt.reshape(bp, 1, nh, HEAD_DIM, lp), (0, 1, 4, 2, 3))
    new_v_prompt = jnp.transpose(vt.reshape(bp, 1, nh, HEAD_DIM, lp), (0, 1, 4, 2, 3))
    new_logf_prompt = jnp.transpose(lft.reshape(bp, 1, nh, lp), (0, 1, 3, 2))

    ts = bs * ls
    xs = x_sample.reshape(ts, d)
    proj = _norm_linear(xs, nmix[0], fox_wt_pad)
    proj3 = proj.reshape(bs, ls, -1)
    ft3 = jnp.transpose(proj3[:, :, 3 * d:3 * d + nh], (0, 2, 1))
    kt_cache = jnp.transpose(cache_k, (0, 1, 3, 4, 2)).reshape(cache_k.shape[0], cache_k.shape[1], d, page)
    vt_cache = jnp.transpose(cache_v, (0, 1, 3, 4, 2)).reshape(cache_v.shape[0], cache_v.shape[1], d, page)
    lf_cache = jnp.transpose(cache_logf, (0, 1, 3, 2))
    o_s, lf_s = _attn_sample(page_table, proj3, ft3, bf_row, bf_col, utri, hmask,
                             kt_cache, vt_cache, lf_cache, d=d)
    xs = _mlp_block(xs, o_s.reshape(ts, d), fox_wo, nmlp[0], wup[0], wdn[0], nout, final_norm=False)
    h0_s = state_ssm[:, 0].reshape(bs, nsh * HEAD_DIM, SSD_STATE)
    tail_s = jnp.concatenate([jnp.zeros((bs, 5, nzx - di), F32), state_conv[:, 0]], axis=1)
    y_s, ssm_s, conv_s = ssd_layer(xs, bs, ls, h0_s, tail_s)
    y_sample = y_s.reshape(bs, ls, d)
    new_k_sample = proj3[:, :, d:2 * d].reshape(bs, 1, ls, nh, HEAD_DIM)
    new_v_sample = proj3[:, :, 2 * d:3 * d].reshape(bs, 1, ls, nh, HEAD_DIM)
    new_logf_sample = lf_s.reshape(bs, 1, ls, nh)

    return (y_prompt, y_sample, new_k_prompt, new_v_prompt, new_logf_prompt, ssm_p, conv_p,
            new_k_sample, new_v_sample, new_logf_sample, ssm_s, conv_s)
```

```python
import functools

import jax
import jax.numpy as jnp
from jax import lax
from jax.experimental import pallas as pl
from jax.experimental.pallas import tpu as pltpu

F32 = jnp.float32
BF16 = jnp.bfloat16
EPS = 1e-5
LANES = 128
HEAD_DIM = 64
SSD_STATE = 128
SSD_GROUP_W = 256
VMEM_LIMIT = 52 * 1024 * 1024

NT = (((1,), (1,)), ((), ()))
NN = (((1,), (0,)), ((), ()))
TN = (((0,), (0,)), ((), ()))


def _cparams(sem):
    return pltpu.CompilerParams(dimension_semantics=sem, vmem_limit_bytes=VMEM_LIMIT)


def _dot(a, b, dims=NN):
    return lax.dot_general(a, b, dims, preferred_element_type=F32)


def _split3(x):
    hi = x.astype(BF16)
    r = x - hi.astype(F32)
    mid = r.astype(BF16)
    lo = (r - mid.astype(F32)).astype(BF16)
    return hi, mid, lo


def _dot_f32_lhs(x, sel, dims=NN):
    hi, mid, lo = _split3(x)
    return _dot(hi, sel, dims) + _dot(mid, sel, dims) + _dot(lo, sel, dims)


def _dot_f32_rhs(sel, x, dims=NN):
    hi, mid, lo = _split3(x)
    return _dot(sel, hi, dims) + _dot(sel, mid, dims) + _dot(sel, lo, dims)


def _rmsnorm(x, w):
    ms = jnp.mean(x * x, axis=-1, keepdims=True)
    return x * lax.rsqrt(ms + EPS) * w


def _softplus(x):
    return jnp.maximum(x, 0.0) + jnp.log1p(jnp.exp(-jnp.abs(x)))


def _log_sigmoid(x):
    return -_softplus(-x)


def _silu(x):
    hx = 0.5 * x
    return hx + hx * jnp.tanh(hx)


def _fox_in_prompt_kernel(x_ref, nw_ref, wq_ref, wk_ref, wv_ref, wf_ref, bf_ref,
                          q_ref, kt_ref, vt_ref, ktb_ref, vtb_ref, lft_ref, *, scale, nchunk):
    xn = _rmsnorm(x_ref[...], nw_ref[...]).astype(BF16)
    d = xn.shape[1]
    cw = d // nchunk
    for c in range(nchunk):
        sl = slice(c * cw, (c + 1) * cw)
        q = _dot(xn, wq_ref[sl, :], NT)
        q_ref[:, sl] = (q * scale).astype(BF16)
        kt = _dot(wk_ref[sl, :], xn, NT)
        kt_ref[0, sl, :] = kt
        ktb_ref[0, sl, :] = kt.astype(BF16)
        vt = _dot(wv_ref[sl, :], xn, NT)
        vt_ref[0, sl, :] = vt
        vtb_ref[0, sl, :] = vt.astype(BF16)
    ft = _dot(wf_ref[...], xn, NT)
    lft_ref[0] = _log_sigmoid(ft + bf_ref[...])


def _fox_in_prompt(x, nw, wt, bf_col, *, batch, seq, tm=512):
    t, d = x.shape
    nh = bf_col.shape[0]
    nt = seq // tm
    scale = HEAD_DIM ** -0.5
    row = lambda i: (i, 0)
    full = lambda i: (0, 0)
    tr = lambda i: (i // nt, 0, i % nt)
    kern = functools.partial(_fox_in_prompt_kernel, scale=scale, nchunk=4)
    return pl.pallas_call(
        kern,
        grid=(t // tm,),
        in_specs=[pl.BlockSpec((tm, d), row), pl.BlockSpec((1, d), full),
                  pl.BlockSpec((d, d), lambda i: (0, 0)), pl.BlockSpec((d, d), lambda i: (1, 0)),
                  pl.BlockSpec((d, d), lambda i: (2, 0)),
                  pl.BlockSpec((nh, d), lambda i: ((3 * d) // nh, 0)), pl.BlockSpec((nh, 1), full)],
        out_specs=[pl.BlockSpec((tm, d), row),
                   pl.BlockSpec((1, d, tm), tr), pl.BlockSpec((1, d, tm), tr),
                   pl.BlockSpec((1, d, tm), tr), pl.BlockSpec((1, d, tm), tr),
                   pl.BlockSpec((1, nh, tm), tr)],
        out_shape=[jax.ShapeDtypeStruct((t, d), BF16),
                   jax.ShapeDtypeStruct((batch, d, seq), F32), jax.ShapeDtypeStruct((batch, d, seq), F32),
                   jax.ShapeDtypeStruct((batch, d, seq), BF16), jax.ShapeDtypeStruct((batch, d, seq), BF16),
                   jax.ShapeDtypeStruct((batch, nh, seq), F32)],
        compiler_params=_cparams(("parallel",)),
        name="fox_in_prompt",
    )(x, nw, wt, wt, wt, wt, bf_col)


def _cumsum_kernel(lf_ref, u_ref, c_ref, *, chunk):
    nh, seq = lf_ref.shape[1], lf_ref.shape[2]
    carry = jnp.zeros((nh, 1), F32)
    for j in range(seq // chunk):
        sl = slice(j * chunk, (j + 1) * chunk)
        c = _dot_f32_lhs(lf_ref[0, :, sl], u_ref[...]) + carry
        c_ref[0, :, sl] = c
        carry = c[:, chunk - 1:chunk]


def _cumsum_lanes(lft, utri):
    b, nh, seq = lft.shape
    chunk = utri.shape[0]
    return pl.pallas_call(
        functools.partial(_cumsum_kernel, chunk=chunk),
        grid=(b,),
        in_specs=[pl.BlockSpec((1, nh, seq), lambda i: (i, 0, 0)),
                  pl.BlockSpec((chunk, chunk), lambda i: (0, 0))],
        out_specs=pl.BlockSpec((1, nh, seq), lambda i: (i, 0, 0)),
        out_shape=jax.ShapeDtypeStruct((b, nh, seq), F32),
        compiler_params=_cparams(("parallel",)),
        name="fox_cumsum",
    )(lft, utri)


N_SPLIT = 3


def _attn_prompt_kernel(q_ref, kt_ref, vt_ref, c_ref, o_ref, m_sc, acc_sc, *, blk, wide, dsplit):
    i = pl.program_id(2)
    q2 = q_ref[...].astype(F32)
    lane = lax.broadcasted_iota(jnp.int32, (blk, LANES), 1)
    lo = lane < HEAD_DIM
    qe = [jnp.where(lo, q2, jnp.where(lane < HEAD_DIM + N_SPLIT, 1.0, 0.0)).astype(BF16),
          jnp.where(lo, jnp.where(lane < N_SPLIT, 1.0, 0.0), q2).astype(BF16)]
    m_sc[...] = jnp.full(m_sc.shape, -jnp.inf, F32)
    acc_sc[...] = jnp.zeros(acc_sc.shape, F32)
    hb = blk // dsplit
    consts = {}
    for wk in sorted({blk, wide * blk} | {(t + 1) * hb for t in range(dsplit)}):
        rowi = lax.broadcasted_iota(jnp.int32, (16, wk), 0)
        pad = jnp.zeros((HEAD_DIM - 16, wk), BF16)
        ones_blk = jnp.concatenate([jnp.where(rowi == 0, 1.0, 0.0).astype(BF16), pad], axis=0)
        consts[wk] = (rowi, pad, ones_blk)

    def step(off, wk, r0=0, nr=blk, shift=None):
        rowi, pad, ones_blk = consts[wk]
        nrep = wk // LANES
        rows = slice(r0, r0 + nr)
        ck = c_ref[0, 0, :, pl.ds(off, wk)]
        if shift is not None:
            r = lax.broadcasted_iota(jnp.int32, (nr, wk), 0)
            cidx = lax.broadcasted_iota(jnp.int32, (nr, wk), 1)
            vis = cidx <= r + shift
        scores, vtes = [], []
        for h in range(2):
            hs = slice(h * HEAD_DIM, (h + 1) * HEAD_DIM)
            hi, mid, low = _split3(-ck[h:h + 1, :])
            b16 = jnp.where(rowi == 0, hi.astype(F32),
                            jnp.where(rowi == 1, mid.astype(F32),
                                      jnp.where(rowi == 2, low.astype(F32), 0.0))).astype(BF16)
            bias_blk = jnp.concatenate([b16, pad], axis=0)
            kth = kt_ref[0, hs, pl.ds(off, wk)]
            vth = vt_ref[0, hs, pl.ds(off, wk)]
            if h == 0:
                kte = jnp.concatenate([kth, bias_blk], axis=0)
                vtes.append(jnp.concatenate([vth, ones_blk], axis=0))
            else:
                kte = jnp.concatenate([bias_blk, kth], axis=0)
                vtes.append(jnp.concatenate([ones_blk, vth], axis=0))
            s = _dot(qe[h][rows], kte)
            if shift is not None:
                s = jnp.where(vis, s, -jnp.inf)
            scores.append(s)
        probs, alphas = [], []
        for h in range(2):
            m_prev = m_sc[h, rows, :]
            m_new = jnp.maximum(m_prev, jnp.max(scores[h], axis=-1, keepdims=True))
            alphas.append(jnp.exp(m_prev - m_new))
            probs.append(jnp.exp(scores[h] - jnp.concatenate([m_new] * nrep, axis=1)).astype(BF16))
            m_sc[h, rows, :] = m_new
        for h in range(2):
            pv = _dot(probs[h], vtes[h], NT)
            acc_sc[h, rows, :] = alphas[h] * acc_sc[h, rows, :] + pv

    def body(j, carry):
        step(pl.multiple_of(j * (wide * blk), wide * blk), wide * blk)
        return carry

    nwide = i // wide
    lax.fori_loop(0, nwide, body, 0)
    for extra in range(wide - 1):
        @pl.when(nwide * wide + extra < i)
        def _():
            step(pl.multiple_of((nwide * wide + extra) * blk, blk), blk)
    for t in range(dsplit):
        step(pl.multiple_of(i * blk, blk), (t + 1) * hb, r0=t * hb, nr=hb, shift=t * hb)
    a0 = acc_sc[0]
    a1 = acc_sc[1]
    out0 = a0 / a0[:, HEAD_DIM:HEAD_DIM + 1]
    out1 = a1 / a1[:, 0:1]
    o_ref[...] = jnp.where(lo, out0, out1).astype(o_ref.dtype)


def _attn_prompt(q, ktb, vtb, c4, *, batch, seq, blk=512, wide=2, dsplit=1):
    t, d = q.shape
    npair = d // LANES
    nq = seq // blk
    qmap = lambda b, h, i: (b * nq + i, h)
    kmap = lambda b, h, i: (b, h, 0)
    return pl.pallas_call(
        functools.partial(_attn_prompt_kernel, blk=blk, wide=wide, dsplit=dsplit),
        grid=(batch, npair, nq),
        in_specs=[pl.BlockSpec((blk, LANES), qmap),
                  pl.BlockSpec((1, LANES, seq), kmap), pl.BlockSpec((1, LANES, seq), kmap),
                  pl.BlockSpec((1, 1, 2, seq), lambda b, h, i: (b, h, 0, 0))],
        out_specs=pl.BlockSpec((blk, LANES), qmap),
        out_shape=jax.ShapeDtypeStruct((t, d), BF16),
        scratch_shapes=[pltpu.VMEM((2, blk, LANES), F32), pltpu.VMEM((2, blk, LANES), F32)],
        compiler_params=_cparams(("parallel", "parallel", "arbitrary")),
        name="fox_attn_prompt",
    )(q, ktb, vtb, c4)


def _mlp_kernel(x_ref, a_ref, wo_ref, nw_ref, wup_ref, wdn_ref, fw_ref, o_ref, xn_sc, acc_sc, *, final_norm):
    f = pl.program_id(1)

    @pl.when(f == 0)
    def _():
        x = x_ref[...] + _dot(a_ref[...].astype(BF16), wo_ref[...])
        xn_sc[...] = _rmsnorm(x, nw_ref[...]).astype(BF16)
        acc_sc[...] = x

    h = jnp.maximum(_dot(xn_sc[...], wup_ref[...]), 0.0)
    acc_sc[...] += _dot((h * h).astype(BF16), wdn_ref[...])

    @pl.when(f == pl.num_programs(1) - 1)
    def _():
        y = acc_sc[...]
        if final_norm:
            y = _rmsnorm(y, fw_ref[...])
        o_ref[...] = y


def _mlp_block(x, a, wo, nw, wup, wdn, fw, *, final_norm, tm=1024, tf=1024):
    t, d = x.shape
    k = a.shape[1]
    dff = wup.shape[1]
    tm = min(tm, t)
    return pl.pallas_call(
        functools.partial(_mlp_kernel, final_norm=final_norm),
        grid=(t // tm, dff // tf),
        in_specs=[pl.BlockSpec((tm, d), lambda i, f: (i, 0)), pl.BlockSpec((tm, k), lambda i, f: (i, 0)),
                  pl.BlockSpec((k, d), lambda i, f: (0, 0)), pl.BlockSpec((1, d), lambda i, f: (0, 0)),
                  pl.BlockSpec((d, tf), lambda i, f: (0, f)), pl.BlockSpec((tf, d), lambda i, f: (f, 0)),
                  pl.BlockSpec((1, d), lambda i, f: (0, 0))],
        out_specs=pl.BlockSpec((tm, d), lambda i, f: (i, 0)),
        out_shape=jax.ShapeDtypeStruct((t, d), F32),
        scratch_shapes=[pltpu.VMEM((tm, d), BF16), pltpu.VMEM((tm, d), F32)],
        compiler_params=_cparams(("parallel", "arbitrary")),
        name="mlp_block",
    )(x, a, wo, nw, wup, wdn, fw)


def _norm_linear_kernel(x_ref, nw_ref, wt_ref, o_ref):
    xn = _rmsnorm(x_ref[...], nw_ref[...]).astype(BF16)
    o_ref[...] = _dot(xn, wt_ref[...], NT)


def _norm_linear(x, nw, wt, *, tm=128):
    t, d = x.shape
    n = wt.shape[0]
    return pl.pallas_call(
        _norm_linear_kernel,
        grid=(t // tm,),
        in_specs=[pl.BlockSpec((tm, d), lambda i: (i, 0)), pl.BlockSpec((1, d), lambda i: (0, 0)),
                  pl.BlockSpec((n, d), lambda i: (0, 0))],
        out_specs=pl.BlockSpec((tm, n), lambda i: (i, 0)),
        out_shape=jax.ShapeDtypeStruct((t, n), F32),
        compiler_params=_cparams(("parallel",)),
        name="norm_linear",
    )(x, nw, wt)


def _ssd_mixer_kernel(*refs, q, valid, nheads, ncol, pipe, preproj):
    nlead = 5 if preproj else 7
    (h0_ref, tx_ref, tbc_ref,
     cwx_ref, cwbc_ref, cbx_ref, cbbc_ref, arow_ref, acol_ref, drow_ref, nw_ref,
     ltri_ref, utri_ref,
     gy_ref, hout_ref, txo_ref, tbco_ref,
     h_sc, xpad_sc, bcpad_sc, z_sc, dtn_sc, dtt_sc, x_sc,
     sz_sc, sx_sc, sbc_sc, sdtn_sc, sdtt_sc) = refs[nlead:]
    c = pl.program_id(1)
    di = z_sc.shape[1]
    ngroups = di // SSD_GROUP_W
    first_scan = 1 if pipe else 0

    @pl.when(c == 0)
    def _():
        if pipe:
            for ref in (sz_sc, sx_sc, sbc_sc, sdtn_sc, sdtt_sc):
                ref[...] = jnp.zeros(ref.shape, F32)
        xpad_sc[0:8, :] = tx_ref[...]
        bcpad_sc[0:8, :] = tbc_ref[...]

    @pl.when(c <= first_scan)
    def _():
        h_sc[...] = h0_ref[...]

    if pipe:
        z_sc[...] = sz_sc[...]
        xpad_sc[8:8 + q, :] = sx_sc[...]
        bcpad_sc[8:8 + q, :] = sbc_sc[...]
        dtn_sc[...] = sdtn_sc[...]
        dtt_sc[...] = sdtt_sc[...]
        pz, px, pbc, pdtn, pdtt, prow = sz_sc, sx_sc, sbc_sc, sdtn_sc, sdtt_sc, 0
    else:
        pz, px, pbc, pdtn, pdtt, prow = z_sc, xpad_sc, bcpad_sc, dtn_sc, dtt_sc, 8

    if preproj:
        zin_ref, xin_ref, bcin_ref, dtnin_ref, dttin_ref = refs[:nlead]
        assert not pipe
        for dst, src, r0 in ((z_sc, zin_ref, 0), (xpad_sc, xin_ref, 8), (bcpad_sc, bcin_ref, 8),
                             (dtn_sc, dtnin_ref, 0)):
            if valid != q:
                dst[r0:r0 + q, :] = jnp.zeros((q, dst.shape[1]), F32)
            dst[r0:r0 + valid, :] = src[...]
        if valid != q:
            dtt_sc[...] = jnp.zeros(dtt_sc.shape, F32)
        dtt_sc[:, 0:valid] = dttin_ref[...]
    else:
        x_ref, nwx_ref, wzx_ref, wdtp_ref, wdt_ref, brow_ref, bcol_ref = refs[:nlead]
        if valid == q:
            x = x_ref[...]
        else:
            x_sc[...] = jnp.zeros(x_sc.shape, F32)
            x_sc[0:valid, :] = x_ref[...]
            x = x_sc[...]
        xn = _rmsnorm(x, nwx_ref[...]).astype(BF16)
        dtn_p = _softplus(_dot(xn, wdtp_ref[...], NT) + brow_ref[...])
        dtt_p = _softplus(_dot(wdt_ref[...], xn, NT) + bcol_ref[...])
        if valid != q:
            dtn_p = jnp.where(lax.broadcasted_iota(jnp.int32, dtn_p.shape, 0) < valid, dtn_p, 0.0)
            dtt_p = jnp.where(lax.broadcasted_iota(jnp.int32, dtt_p.shape, 1) < valid, dtt_p, 0.0)
        pdtn[...] = dtn_p
        pdtt[...] = dtt_p
        cw = (3 * di) // ncol

        def project(k):
            piece = _dot(xn, wzx_ref[k * cw:(k + 1) * cw, :], NT)
            tgt, off = divmod(k * cw, di)
            if tgt == 0:
                pz[:, off:off + cw] = piece
            elif tgt == 1:
                px[prow:prow + q, off:off + cw] = piece
            else:
                pbc[prow:prow + q, off:off + cw] = piece

        if not pipe:
            for k in range(ncol):
                project(k)

    def conv(pad_sc, w_ref, b_ref, sl):
        w = w_ref[:, sl]
        ext = pad_sc[:, sl]
        acc = b_ref[:, sl] + ext[8:8 + q] * w[3:4, :]
        for s in range(1, 4):
            acc = acc + pltpu.roll(ext, s, axis=0)[8:8 + q] * w[3 - s:4 - s, :]
        return _silu(acc)

    dtn = dtn_sc[...]
    dtt = dtt_sc[...]
    a_nat = dtn * arow_ref[...]
    a_t = dtt * acol_ref[...]
    cum_nat = _dot_f32_rhs(ltri_ref[...], a_nat)
    cum_t = _dot_f32_lhs(a_t, utri_ref[...])
    ri = lax.broadcasted_iota(jnp.int32, (q, q), 0)
    ci = lax.broadcasted_iota(jnp.int32, (q, q), 1)
    tri = ci <= ri
    lo = lax.broadcasted_iota(jnp.int32, (q, LANES), 1) < HEAD_DIM
    rlo = lax.broadcasted_iota(jnp.int32, (LANES, SSD_STATE), 0) < HEAD_DIM
    nb = di // 2
    clast_row = cum_nat[q - 1:q, :]
    e_nat = jnp.exp(cum_nat)
    w_nat = jnp.exp(clast_row - cum_nat) * dtn
    dl_row = jnp.exp(clast_row)

    for g in range(ngroups):
        if pipe:
            for k in range(g * ncol // ngroups, (g + 1) * ncol // ngroups):
                project(k)
        bg = conv(bcpad_sc, cwbc_ref, cbbc_ref, slice(g * SSD_STATE, (g + 1) * SSD_STATE))
        cg = conv(bcpad_sc, cwbc_ref, cbbc_ref, slice(nb + g * SSD_STATE, nb + (g + 1) * SSD_STATE))
        bb = bg.astype(BF16)
        cb16 = cg.astype(BF16)
        cb = _dot(cb16, bb, NT)
        gated = []
        ssq = jnp.zeros((q, 1), F32)
        for pr in range(2):
            p = 2 * g + pr
            sl = slice(p * LANES, (p + 1) * LANES)
            xp = conv(xpad_sc, cwx_ref, cbx_ref, sl)
            xpb = xp.astype(BF16)
            yi, ee, we, dl = [], [], [], []
            for r in range(2):
                h = 2 * p + r
                colb = jnp.broadcast_to(cum_nat[:, h:h + 1], (q, q))
                rowb = cum_t[h:h + 1, :]
                dec = jnp.exp(jnp.where(tri, colb - rowb, -jnp.inf))
                w = (cb * dec * dtt[h:h + 1, :]).astype(BF16)
                yi.append(_dot(w, xpb))
                ee.append(jnp.broadcast_to(e_nat[:, h:h + 1], (q, LANES)))
                we.append(jnp.broadcast_to(w_nat[:, h:h + 1], (q, LANES)))
                dl.append(jnp.broadcast_to(dl_row[:, h:h + 1], (LANES, SSD_STATE)))
            hp = h_sc[sl, :]
            y_inter = _dot(cb16, hp.astype(BF16), NT) * jnp.where(lo, ee[0], ee[1])
            xw = (xp * jnp.where(lo, we[0], we[1])).astype(BF16)
            h_sc[sl, :] = jnp.where(rlo, dl[0], dl[1]) * hp + _dot(xw, bb, TN)
            y = jnp.where(lo, yi[0], yi[1]) + y_inter + xp * drow_ref[:, sl]
            gt = y * _silu(z_sc[:, sl])
            ssq = ssq + jnp.sum(gt * gt, axis=-1, keepdims=True)
            gated.append(gt)
        rs = lax.rsqrt(ssq / SSD_GROUP_W + EPS)
        for pr in range(2):
            sl = slice((2 * g + pr) * LANES, (2 * g + pr + 1) * LANES)
            gy_ref[:, sl] = (gated[pr] * rs * nw_ref[:, sl])[0:valid].astype(gy_ref.dtype)

    @pl.when(c == pl.num_programs(1) - 1)
    def _():
        hout_ref[...] = h_sc[...]
        txo_ref[...] = xpad_sc[valid:valid + 8, :]
        tbco_ref[...] = bcpad_sc[valid:valid + 8, :]

    @pl.when(c >= first_scan)
    def _():
        xpad_sc[0:8, :] = xpad_sc[q:q + 8, :]
        bcpad_sc[0:8, :] = bcpad_sc[q:q + 8, :]


def _ssd_proj_kernel(x_ref, nw_ref, w_ref, wdtp_ref, wdt_ref, brow_ref, bcol_ref, zx_ref, dtn_ref, dtt_ref):
    xn = _rmsnorm(x_ref[...], nw_ref[...]).astype(BF16)
    dtn_ref[...] = _softplus(_dot(xn, wdtp_ref[...], NT) + brow_ref[...])
    dtt_ref[...] = _softplus(_dot(wdt_ref[...], xn, NT) + bcol_ref[...])
    zx_ref[...] = _dot(xn, w_ref[...], NT)


def _ssd_proj(x, nw, wt, wdtp, brow, bcol, *, tn=1024):
    t, d = x.shape
    nh = bcol.shape[0]
    n = wt.shape[0] - nh
    c0 = lambda j: (0, 0)
    return pl.pallas_call(
        _ssd_proj_kernel,
        grid=(n // tn,),
        in_specs=[pl.BlockSpec((t, d), c0), pl.BlockSpec((1, d), c0), pl.BlockSpec((tn, d), lambda j: (j, 0)),
                  pl.BlockSpec((LANES, d), c0), pl.BlockSpec((nh, d), lambda j: (n // nh, 0)),
                  pl.BlockSpec((1, LANES), c0), pl.BlockSpec((nh, 1), c0)],
        out_specs=[pl.BlockSpec((t, tn), lambda j: (0, j)), pl.BlockSpec((t, LANES), c0),
                   pl.BlockSpec((nh, t), c0)],
        out_shape=[jax.ShapeDtypeStruct((t, n), F32), jax.ShapeDtypeStruct((t, LANES), F32),
                   jax.ShapeDtypeStruct((nh, t), F32)],
        compiler_params=_cparams(("arbitrary",)),
        name="ssd_proj",
    )(x, nw, wt, wdtp, wt, brow, bcol)


def _ssd_mixer(x3, nwx, wt, wdtp, brow, bcol, h0, tail, cw, cb, arow, acol, drow, nw,
               *, chunk=256, short_chunk=16, ncol=6):
    b, seq, d = x3.shape
    q = chunk if seq >= chunk else short_chunk
    utri = _tri_upper(q)
    ltri = jnp.transpose(utri)
    nheads = bcol.shape[0]
    di = nheads * HEAD_DIM
    valid = min(q, seq)
    nc = max(seq // q, 1)
    pipe = nc > 1
    preproj = not pipe
    nsteps = nc + 1 if pipe else nc
    hrows = nheads * HEAD_DIM
    const2 = lambda k: (lambda i, c: (0, k))
    stage = [pltpu.VMEM((q, di), F32)] * 3 + [pltpu.VMEM((q, LANES), F32), pltpu.VMEM((nheads, q), F32)]
    if not pipe:
        stage = [pltpu.VMEM((8, LANES), F32)] * 5
    if pipe:
        ncol = 3 * (di // SSD_GROUP_W)
    if preproj:
        zx, dtn, dtt = _ssd_proj(x3.reshape(b * seq, d), nwx, wt, wdtp, brow, bcol)
        zx3 = zx.reshape(b, seq, 3 * di)
        lead = [zx3, zx3, zx3, dtn.reshape(b, seq, LANES),
                jnp.transpose(dtt.reshape(nheads, b, seq), (1, 0, 2))]
        lead_specs = [pl.BlockSpec((None, valid, di), lambda i, c: (i, c, 0)),
                      pl.BlockSpec((None, valid, di), lambda i, c: (i, c, 1)),
                      pl.BlockSpec((None, valid, di), lambda i, c: (i, c, 2)),
                      pl.BlockSpec((None, valid, LANES), lambda i, c: (i, c, 0)),
                      pl.BlockSpec((None, nheads, valid), lambda i, c: (i, 0, c))]
    else:
        lead = [x3, nwx, wt, wdtp, wt, brow, bcol]
        lead_specs = [pl.BlockSpec((None, valid, d), lambda i, c: (i, jnp.minimum(c, nc - 1), 0)),
                      pl.BlockSpec((1, d), const2(0)), pl.BlockSpec((3 * di, d), const2(0)),
                      pl.BlockSpec((LANES, d), const2(0)),
                      pl.BlockSpec((nheads, d), lambda i, c: ((3 * di) // nheads, 0)),
                      pl.BlockSpec((1, LANES), const2(0)), pl.BlockSpec((nheads, 1), const2(0))]
    kern = functools.partial(_ssd_mixer_kernel, q=q, valid=valid, nheads=nheads, ncol=ncol, pipe=pipe,
                             preproj=preproj)
    return pl.pallas_call(
        kern,
        grid=(b, nsteps),
        in_specs=lead_specs + [
                  pl.BlockSpec((None, hrows, SSD_STATE), lambda i, c: (i, 0, 0)),
                  pl.BlockSpec((None, 8, di), lambda i, c: (i, 0, 0)),
                  pl.BlockSpec((None, 8, di), lambda i, c: (i, 0, 1)),
                  pl.BlockSpec((4, di), const2(0)), pl.BlockSpec((4, di), const2(1)),
                  pl.BlockSpec((1, di), const2(0)), pl.BlockSpec((1, di), const2(1)),
                  pl.BlockSpec((1, LANES), const2(0)), pl.BlockSpec((nheads, 1), const2(0)),
                  pl.BlockSpec((1, di), const2(0)), pl.BlockSpec((1, di), const2(0)),
                  pl.BlockSpec((q, q), const2(0)), pl.BlockSpec((q, q), const2(0))],
        out_specs=[pl.BlockSpec((None, valid, di), lambda i, c: (i, jnp.maximum(c - (nsteps - nc), 0), 0)),
                   pl.BlockSpec((None, hrows, SSD_STATE), lambda i, c: (i, 0, 0)),
                   pl.BlockSpec((None, 8, di), lambda i, c: (i, 0, 0)),
                   pl.BlockSpec((None, 8, di), lambda i, c: (i, 0, 0))],
        out_shape=[jax.ShapeDtypeStruct((b, seq, di), BF16 if valid == q else F32),
                   jax.ShapeDtypeStruct((b, hrows, SSD_STATE), F32),
                   jax.ShapeDtypeStruct((b, 8, di), F32), jax.ShapeDtypeStruct((b, 8, di), F32)],
        scratch_shapes=[pltpu.VMEM((hrows, SSD_STATE), F32),
                        pltpu.VMEM((q + 8, di), F32), pltpu.VMEM((q + 8, di), F32),
                        pltpu.VMEM((q, di), F32), pltpu.VMEM((q, LANES), F32), pltpu.VMEM((nheads, q), F32),
                        pltpu.VMEM((q, d), F32)] + stage,
        compiler_params=_cparams(("parallel", "arbitrary")),
        name="ssd_mixer",
    )(*lead, h0, tail, tail, cw, cw, cb, cb, arow, acol, drow, nw, ltri, utri)


def _attn_sample_kernel(pt_ref, q_ref, kn_ref, vn_ref, f_ref, ft_ref, bfr_ref, bfc_ref, u_ref, hm_ref,
                        *refs, pg, nq, nh, scale):
    k_refs, v_refs, lf_refs = refs[0:pg], refs[pg:2 * pg], refs[2 * pg:3 * pg]
    o_ref, lfo_ref = refs[3 * pg], refs[3 * pg + 1]
    qbd_sc, m_sc, l_sc, acc_sc, carry_sc, knp_sc, vnp_sc = refs[3 * pg + 2:]
    g = pl.program_id(1)
    rows = nq * nh
    d = acc_sc.shape[1]

    @pl.when(g == 0)
    def _():
        qv = q_ref[...] * scale
        hm = hm_ref[...]
        for qi in range(nq):
            qbd_sc[qi * nh:(qi + 1) * nh, :] = (qv[qi:qi + 1, :] * hm).astype(BF16)
        m_sc[...] = jnp.full(m_sc.shape, -jnp.inf, F32)
        l_sc[...] = jnp.zeros(l_sc.shape, F32)
        acc_sc[...] = jnp.zeros(acc_sc.shape, F32)
        carry_sc[...] = jnp.zeros(carry_sc.shape, F32)

    def update(s, pv_fn):
        m_prev = m_sc[...]
        m_new = jnp.maximum(m_prev, jnp.max(s, axis=-1, keepdims=True))
        alpha = jnp.exp(m_prev - m_new)
        p = jnp.exp(s - jnp.concatenate([m_new] * (s.shape[1] // LANES), axis=1))
        l_sc[...] = alpha * l_sc[...] + jnp.sum(p, axis=-1, keepdims=True)
        acc_sc[...] = jnp.concatenate([alpha] * (d // LANES), axis=1) * acc_sc[...] + pv_fn(p.astype(BF16))
        m_sc[...] = m_new

    qbd = qbd_sc[...]
    off = carry_sc[...]
    scores = []
    for pi in range(pg):
        loc = _dot_f32_lhs(lf_refs[pi][...], u_ref[...])
        cpage = loc + off
        off = off + loc[:, LANES - 1:LANES]
        kt = k_refs[pi][...].astype(BF16)
        scores.append(_dot(qbd, kt) - jnp.concatenate([cpage] * nq, axis=0))
    carry_sc[...] = off

    def pv_pages(p):
        acc = None
        for pi in range(pg):
            part = _dot(p[:, pi * LANES:(pi + 1) * LANES], v_refs[pi][...].astype(BF16), NT)
            acc = part if acc is None else acc + part
        return acc

    update(jnp.concatenate(scores, axis=1), pv_pages)

    @pl.when(g == pl.num_programs(1) - 1)
    def _():
        lfo_ref[...] = _log_sigmoid(f_ref[:, 0:nh] + bfr_ref[...])
        lft = _log_sigmoid(ft_ref[...] + bfc_ref[...])
        lane = lax.broadcasted_iota(jnp.int32, (nh, LANES), 1)
        cnew = jnp.zeros((nh, LANES), F32)
        run = carry_sc[...]
        for t in range(nq):
            run = run + lft[:, t:t + 1]
            cnew = jnp.where(lane == t, run, cnew)
        knp_sc[...] = jnp.zeros(knp_sc.shape, F32)
        knp_sc[0:nq, :] = kn_ref[...]
        vnp_sc[...] = jnp.zeros(vnp_sc.shape, F32)
        vnp_sc[0:nq, :] = vn_ref[...]
        s = _dot(qbd_sc[...], knp_sc[...].astype(BF16), NT) - jnp.concatenate([cnew] * nq, axis=0)
        key = lax.broadcasted_iota(jnp.int32, (rows, LANES), 1)
        ridx = lax.broadcasted_iota(jnp.int32, (rows, LANES), 0)
        qrow = jnp.zeros((rows, LANES), jnp.int32)
        for qi in range(1, nq):
            qrow = qrow + (ridx >= qi * nh).astype(jnp.int32)
        s = jnp.where(key <= qrow, s, -jnp.inf)
        vnb = vnp_sc[...].astype(BF16)
        update(s, lambda p: _dot(p, vnb))
        out = acc_sc[...] / l_sc[:, 0:1]
        hm = hm_ref[...]
        for qi in range(nq):
            o_ref[qi:qi + 1, :] = jnp.sum(out[qi * nh:(qi + 1) * nh, :] * hm, axis=0,
                                          keepdims=True).astype(o_ref.dtype)


def _attn_sample(page_table, proj3, ft3, bf_row, bf_col, utri, hmask, kt_cache, vt_cache, lf_cache,
                 *, d, pg=16):
    nseq, nq, _ = proj3.shape
    npages = page_table.shape[1]
    nh = hmask.shape[0]
    rows = nq * nh
    scale = HEAD_DIM ** -0.5
    page = kt_cache.shape[-1]
    fblk = (3 * d) // LANES

    def pmap(pi):
        return lambda b, g, pt: (pt[b, g * pg + pi], 0, 0, 0)

    cmap = lambda b, g, pt: (0, 0)
    in_specs = [pl.BlockSpec((None, nq, d), lambda b, g, pt: (b, 0, 0)),
                pl.BlockSpec((None, nq, d), lambda b, g, pt: (b, 0, 1)),
                pl.BlockSpec((None, nq, d), lambda b, g, pt: (b, 0, 2)),
                pl.BlockSpec((None, nq, LANES), lambda b, g, pt: (b, 0, fblk)),
                pl.BlockSpec((None, nh, nq), lambda b, g, pt: (b, 0, 0)),
                pl.BlockSpec((1, nh), cmap), pl.BlockSpec((nh, 1), cmap),
                pl.BlockSpec((page, page), cmap), pl.BlockSpec((nh, d), cmap)]
    in_specs += [pl.BlockSpec((None, None, d, page), pmap(pi)) for pi in range(pg)]
    in_specs += [pl.BlockSpec((None, None, d, page), pmap(pi)) for pi in range(pg)]
    in_specs += [pl.BlockSpec((None, None, nh, page), pmap(pi)) for pi in range(pg)]
    grid_spec = pltpu.PrefetchScalarGridSpec(
        num_scalar_prefetch=1,
        grid=(nseq, npages // pg),
        in_specs=in_specs,
        out_specs=[pl.BlockSpec((None, nq, d), lambda b, g, pt: (b, 0, 0)),
                   pl.BlockSpec((None, nq, nh), lambda b, g, pt: (b, 0, 0))],
        scratch_shapes=[pltpu.VMEM((rows, d), BF16), pltpu.VMEM((rows, LANES), F32),
                        pltpu.VMEM((rows, LANES), F32),
                        pltpu.VMEM((rows, d), F32), pltpu.VMEM((nh, 1), F32),
                        pltpu.VMEM((LANES, d), F32), pltpu.VMEM((LANES, d), F32)])
    kern = functools.partial(_attn_sample_kernel, pg=pg, nq=nq, nh=nh, scale=scale)
    return pl.pallas_call(
        kern,
        grid_spec=grid_spec,
        out_shape=[jax.ShapeDtypeStruct((nseq, nq, d), F32), jax.ShapeDtypeStruct((nseq, nq, nh), F32)],
        compiler_params=_cparams(("parallel", "arbitrary")),
        name="fox_attn_sample",
    )(page_table, proj3, proj3, proj3, proj3, ft3, bf_row, bf_col, utri, hmask,
      *([kt_cache] * pg), *([vt_cache] * pg), *([lf_cache] * pg))


def _tri_upper(n):
    r = lax.broadcasted_iota(jnp.int32, (n, n), 0)
    c = lax.broadcasted_iota(jnp.int32, (n, n), 1)
    return (r <= c).astype(BF16)


def kernel(x_prompt, x_sample, cache_k, cache_v, cache_logf, page_table, state_ssm, state_conv,
           norm_mix_w, norm_mlp_w, norm_out_w, fox_w_in, fox_b_f, fox_w_out,
           ssd_w_in, ssd_conv_w, ssd_conv_b, ssd_dt_bias, ssd_a_log, ssd_d, ssd_norm_w, ssd_w_out,
           mlp_w_up, mlp_w_down):
    bp, lp, d = x_prompt.shape
    bs, ls, _ = x_sample.shape
    nh = fox_b_f.shape[-1]
    nsh = ssd_dt_bias.shape[-1]
    di = nsh * HEAD_DIM
    nzx = ssd_w_in.shape[-1] - nsh
    page = cache_k.shape[2]

    fox_wt = jnp.transpose(fox_w_in[0]).astype(BF16)
    npad = (-fox_wt.shape[0]) % LANES
    fox_wt_pad = jnp.concatenate([fox_wt, jnp.zeros((npad, d), BF16)], axis=0)
    bf_row = fox_b_f[0].reshape(1, nh)
    bf_col = fox_b_f[0].reshape(nh, 1)
    fox_wo = fox_w_out[0].astype(BF16)
    ssd_wt = jnp.transpose(ssd_w_in[0]).astype(BF16)
    wdtp = jnp.concatenate([ssd_wt[nzx:], jnp.zeros((LANES - nsh, d), BF16)], axis=0)
    dtb_row = jnp.concatenate([ssd_dt_bias[0], jnp.zeros((LANES - nsh,), F32)]).reshape(1, LANES)
    dtb_col = ssd_dt_bias[0].reshape(nsh, 1)
    a_neg = -jnp.exp(ssd_a_log[0])
    a_row = jnp.concatenate([a_neg, jnp.zeros((LANES - nsh,), F32)]).reshape(1, LANES)
    a_col = a_neg.reshape(nsh, 1)
    d_row = jnp.repeat(ssd_d[0], HEAD_DIM).reshape(1, di)
    ssd_nw = ssd_norm_w[0].reshape(1, di)
    ssd_wo = ssd_w_out[0].astype(BF16)
    conv_w = ssd_conv_w[0]
    conv_b = ssd_conv_b[0].reshape(1, -1)
    wup = mlp_w_up.astype(BF16)
    wdn = mlp_w_down.astype(BF16)
    nmix = norm_mix_w.reshape(-1, 1, d)
    nmlp = norm_mlp_w.reshape(-1, 1, d)
    nout = norm_out_w.reshape(1, d)
    utri = _tri_upper(LANES)
    utri_c = _tri_upper(256)
    hmask = (lax.broadcasted_iota(jnp.int32, (nh, d), 1) // HEAD_DIM
             == lax.broadcasted_iota(jnp.int32, (nh, d), 0)).astype(F32)

    def ssd_layer(x, batch, seq, h0, tail):
        gy, hlast, tx, tbc = _ssd_mixer(x.reshape(batch, seq, d), nmix[1], ssd_wt, wdtp, dtb_row, dtb_col,
                                        h0, tail, conv_w, conv_b, a_row, a_col, d_row, ssd_nw)
        y = _mlp_block(x, gy.reshape(batch * seq, di), ssd_wo, nmlp[1], wup[1], wdn[1], nout, final_norm=True)
        new_conv = jnp.concatenate([tx[:, 5:], tbc[:, 5:]], axis=-1).reshape(batch, 1, 3, nzx - di)
        return y, hlast.reshape(batch, 1, nsh, HEAD_DIM, SSD_STATE), new_conv

    tp = bp * lp
    xp = x_prompt.reshape(tp, d)
    q, kt, vt, ktb, vtb, lft = _fox_in_prompt(xp, nmix[0], fox_wt, bf_col, batch=bp, seq=lp)
    ct = _cumsum_lanes(lft, utri_c)
    o = _attn_prompt(q, ktb, vtb, ct.reshape(bp, nh // 2, 2, lp), batch=bp, seq=lp)
    xp = _mlp_block(xp, o, fox_wo, nmlp[0], wup[0], wdn[0], nout, final_norm=False)
    h0_p = jnp.zeros((bp, nsh * HEAD_DIM, SSD_STATE), state_ssm.dtype)
    tail_p = jnp.zeros((bp, 8, nzx - di), F32)
    y_p, ssm_p, conv_p = ssd_layer(xp, bp, lp, h0_p, tail_p)
    y_prompt = y_p.reshape(bp, lp, d)
    new_k_prompt = jnp.transpose(kt.reshape(bp, 1, nh, HEAD_DIM, lp), (0, 1, 4, 2, 3))
    new_v_prompt = jnp.transpose(vt.reshape(bp, 1, nh, HEAD_DIM, lp), (0, 1, 4, 2, 3))
    new_logf_prompt = jnp.transpose(lft.reshape(bp, 1, nh, lp), (0, 1, 3, 2))

    ts = bs * ls
    xs = x_sample.reshape(ts, d)
    proj = _norm_linear(xs, nmix[0], fox_wt_pad)
    proj3 = proj.reshape(bs, ls, -1)
    ft3 = jnp.transpose(proj3[:, :, 3 * d:3 * d + nh], (0, 2, 1))
    kt_cache = jnp.transpose(cache_k, (0, 1, 3, 4, 2)).reshape(cache_k.shape[0], cache_k.shape[1], d, page)
    vt_cache = jnp.transpose(cache_v, (0, 1, 3, 4, 2)).reshape(cache_v.shape[0], cache_v.shape[1], d, page)
    lf_cache = jnp.transpose(cache_logf, (0, 1, 3, 2))
    o_s, lf_s = _attn_sample(page_table, proj3, ft3, bf_row, bf_col, utri, hmask,
                             kt_cache, vt_cache, lf_cache, d=d)
    xs = _mlp_block(xs, o_s.reshape(ts, d), fox_wo, nmlp[0], wup[0], wdn[0], nout, final_norm=False)
    h0_s = state_ssm[:, 0].reshape(bs, nsh * HEAD_DIM, SSD_STATE)
    tail_s = jnp.concatenate([jnp.zeros((bs, 5, nzx - di), F32), state_conv[:, 0]], axis=1)
    y_s, ssm_s, conv_s = ssd_layer(xs, bs, ls, h0_s, tail_s)
    y_sample = y_s.reshape(bs, ls, d)
    new_k_sample = proj3[:, :, d:2 * d].reshape(bs, 1, ls, nh, HEAD_DIM)
    new_v_sample = proj3[:, :, 2 * d:3 * d].reshape(bs, 1, ls, nh, HEAD_DIM)
    new_logf_sample = lf_s.reshape(bs, 1, ls, nh)

    return (y_prompt, y_sample, new_k_prompt, new_v_prompt, new_logf_prompt, ssm_p, conv_p,
            new_k_sample, new_v_sample, new_logf_sample, ssm_s, conv_s)
```

```python
import functools

import jax
import jax.numpy as jnp
from jax import lax
from jax.experimental import pallas as pl
from jax.experimental.pallas import tpu as pltpu

F32 = jnp.float32
BF16 = jnp.bfloat16
EPS = 1e-5
LANES = 128
HEAD_DIM = 64
SSD_STATE = 128
SSD_GROUP_W = 256
VMEM_LIMIT = 52 * 1024 * 1024

NT = (((1,), (1,)), ((), ()))
NN = (((1,), (0,)), ((), ()))
TN = (((0,), (0,)), ((), ()))


def _cparams(sem):
    return pltpu.CompilerParams(dimension_semantics=sem, vmem_limit_bytes=VMEM_LIMIT)


def _dot(a, b, dims=NN):
    return lax.dot_general(a, b, dims, preferred_element_type=F32)


def _split3(x):
    hi = x.astype(BF16)
    r = x - hi.astype(F32)
    mid = r.astype(BF16)
    lo = (r - mid.astype(F32)).astype(BF16)
    return hi, mid, lo


def _dot_f32_lhs(x, sel, dims=NN):
    hi, mid, lo = _split3(x)
    return _dot(hi, sel, dims) + _dot(mid, sel, dims) + _dot(lo, sel, dims)


def _dot_f32_rhs(sel, x, dims=NN):
    hi, mid, lo = _split3(x)
    return _dot(sel, hi, dims) + _dot(sel, mid, dims) + _dot(sel, lo, dims)


def _rmsnorm(x, w):
    ms = jnp.mean(x * x, axis=-1, keepdims=True)
    return x * lax.rsqrt(ms + EPS) * w


def _softplus(x):
    return jnp.maximum(x, 0.0) + jnp.log1p(jnp.exp(-jnp.abs(x)))


def _log_sigmoid(x):
    return -_softplus(-x)


def _silu(x):
    hx = 0.5 * x
    return hx + hx * jnp.tanh(hx)


def _fox_in_prompt_kernel(x_ref, nw_ref, wq_ref, wk_ref, wv_ref, wf_ref, bf_ref,
                          q_ref, kt_ref, vt_ref, ktb_ref, vtb_ref, lft_ref, *, scale, nchunk):
    xn = _rmsnorm(x_ref[...], nw_ref[...]).astype(BF16)
    d = xn.shape[1]
    cw = d // nchunk
    for c in range(nchunk):
        sl = slice(c * cw, (c + 1) * cw)
        q = _dot(xn, wq_ref[sl, :], NT)
        q_ref[:, sl] = (q * scale).astype(BF16)
        kt = _dot(wk_ref[sl, :], xn, NT)
        kt_ref[0, sl, :] = kt
        ktb_ref[0, sl, :] = kt.astype(BF16)
        vt = _dot(wv_ref[sl, :], xn, NT)
        vt_ref[0, sl, :] = vt
        vtb_ref[0, sl, :] = vt.astype(BF16)
    ft = _dot(wf_ref[...], xn, NT)
    lft_ref[0] = _log_sigmoid(ft + bf_ref[...])


def _fox_in_prompt(x, nw, wt, bf_col, *, batch, seq, tm=512):
    t, d = x.shape
    nh = bf_col.shape[0]
    nt = seq // tm
    scale = HEAD_DIM ** -0.5
    row = lambda i: (i, 0)
    full = lambda i: (0, 0)
    tr = lambda i: (i // nt, 0, i % nt)
    kern = functools.partial(_fox_in_prompt_kernel, scale=scale, nchunk=4)
    return pl.pallas_call(
        kern,
        grid=(t // tm,),
        in_specs=[pl.BlockSpec((tm, d), row), pl.BlockSpec((1, d), full),
                  pl.BlockSpec((d, d), lambda i: (0, 0)), pl.BlockSpec((d, d), lambda i: (1, 0)),
                  pl.BlockSpec((d, d), lambda i: (2, 0)),
                  pl.BlockSpec((nh, d), lambda i: ((3 * d) // nh, 0)), pl.BlockSpec((nh, 1), full)],
        out_specs=[pl.BlockSpec((tm, d), row),
                   pl.BlockSpec((1, d, tm), tr), pl.BlockSpec((1, d, tm), tr),
                   pl.BlockSpec((1, d, tm), tr), pl.BlockSpec((1, d, tm), tr),
                   pl.BlockSpec((1, nh, tm), tr)],
        out_shape=[jax.ShapeDtypeStruct((t, d), BF16),
                   jax.ShapeDtypeStruct((batch, d, seq), F32), jax.ShapeDtypeStruct((batch, d, seq), F32),
                   jax.ShapeDtypeStruct((batch, d, seq), BF16), jax.ShapeDtypeStruct((batch, d, seq), BF16),
                   jax.ShapeDtypeStruct((batch, nh, seq), F32)],
        compiler_params=_cparams(("parallel",)),
        name="fox_in_prompt",
    )(x, nw, wt, wt, wt, wt, bf_col)


def _cumsum_kernel(lf_ref, u_ref, c_ref, *, chunk):
    nh, seq = lf_ref.shape[1], lf_ref.shape[2]
    carry = jnp.zeros((nh, 1), F32)
    for j in range(seq // chunk):
        sl = slice(j * chunk, (j + 1) * chunk)
        c = _dot_f32_lhs(lf_ref[0, :, sl], u_ref[...]) + carry
        c_ref[0, :, sl] = c
        carry = c[:, chunk - 1:chunk]


def _cumsum_lanes(lft, utri):
    b, nh, seq = lft.shape
    chunk = utri.shape[0]
    return pl.pallas_call(
        functools.partial(_cumsum_kernel, chunk=chunk),
        grid=(b,),
        in_specs=[pl.BlockSpec((1, nh, seq), lambda i: (i, 0, 0)),
                  pl.BlockSpec((chunk, chunk), lambda i: (0, 0))],
        out_specs=pl.BlockSpec((1, nh, seq), lambda i: (i, 0, 0)),
        out_shape=jax.ShapeDtypeStruct((b, nh, seq), F32),
        compiler_params=_cparams(("parallel",)),
        name="fox_cumsum",
    )(lft, utri)


N_SPLIT = 3


def _attn_prompt_kernel(q_ref, kt_ref, vt_ref, c_ref, o_ref, m_sc, acc_sc, *, blk, wide, dsplit):
    i = pl.program_id(2)
    q2 = q_ref[...].astype(F32)
    lane = lax.broadcasted_iota(jnp.int32, (blk, LANES), 1)
    lo = lane < HEAD_DIM
    qe = [jnp.where(lo, q2, jnp.where(lane < HEAD_DIM + N_SPLIT, 1.0, 0.0)).astype(BF16),
          jnp.where(lo, jnp.where(lane < N_SPLIT, 1.0, 0.0), q2).astype(BF16)]
    m_sc[...] = jnp.full(m_sc.shape, -jnp.inf, F32)
    acc_sc[...] = jnp.zeros(acc_sc.shape, F32)
    hb = blk // dsplit
    consts = {}
    for wk in sorted({blk, wide * blk} | {(t + 1) * hb for t in range(dsplit)}):
        rowi = lax.broadcasted_iota(jnp.int32, (16, wk), 0)
        pad = jnp.zeros((HEAD_DIM - 16, wk), BF16)
        ones_blk = jnp.concatenate([jnp.where(rowi == 0, 1.0, 0.0).astype(BF16), pad], axis=0)
        consts[wk] = (rowi, pad, ones_blk)

    def step(off, wk, r0=0, nr=blk, shift=None):
        rowi, pad, ones_blk = consts[wk]
        nrep = wk // LANES
        rows = slice(r0, r0 + nr)
        ck = c_ref[0, 0, :, pl.ds(off, wk)]
        if shift is not None:
            r = lax.broadcasted_iota(jnp.int32, (nr, wk), 0)
            cidx = lax.broadcasted_iota(jnp.int32, (nr, wk), 1)
            vis = cidx <= r + shift
        scores, vtes = [], []
        for h in range(2):
            hs = slice(h * HEAD_DIM, (h + 1) * HEAD_DIM)
            hi, mid, low = _split3(-ck[h:h + 1, :])
            b16 = jnp.where(rowi == 0, hi.astype(F32),
                            jnp.where(rowi == 1, mid.astype(F32),
                                      jnp.where(rowi == 2, low.astype(F32), 0.0))).astype(BF16)
            bias_blk = jnp.concatenate([b16, pad], axis=0)
            kth = kt_ref[0, hs, pl.ds(off, wk)]
            vth = vt_ref[0, hs, pl.ds(off, wk)]
            if h == 0:
                kte = jnp.concatenate([kth, bias_blk], axis=0)
                vtes.append(jnp.concatenate([vth, ones_blk], axis=0))
            else:
                kte = jnp.concatenate([bias_blk, kth], axis=0)
                vtes.append(jnp.concatenate([ones_blk, vth], axis=0))
            s = _dot(qe[h][rows], kte)
            if shift is not None:
                s = jnp.where(vis, s, -jnp.inf)
            scores.append(s)
        probs, alphas = [], []
        for h in range(2):
            m_prev = m_sc[h, rows, :]
            m_new = jnp.maximum(m_prev, jnp.max(scores[h], axis=-1, keepdims=True))
            alphas.append(jnp.exp(m_prev - m_new))
            probs.append(jnp.exp(scores[h] - jnp.concatenate([m_new] * nrep, axis=1)).astype(BF16))
            m_sc[h, rows, :] = m_new
        for h in range(2):
            pv = _dot(probs[h], vtes[h], NT)
            acc_sc[h, rows, :] = alphas[h] * acc_sc[h, rows, :] + pv

    def body(j, carry):
        step(pl.multiple_of(j * (wide * blk), wide * blk), wide * blk)
        return carry

    nwide = i // wide
    lax.fori_loop(0, nwide, body, 0)
    for extra in range(wide - 1):
        @pl.when(nwide * wide + extra < i)
        def _():
            step(pl.multiple_of((nwide * wide + extra) * blk, blk), blk)
    for t in range(dsplit):
        step(pl.multiple_of(i * blk, blk), (t + 1) * hb, r0=t * hb, nr=hb, shift=t * hb)
    a0 = acc_sc[0]
    a1 = acc_sc[1]
    out0 = a0 / a0[:, HEAD_DIM:HEAD_DIM + 1]
    out1 = a1 / a1[:, 0:1]
    o_ref[...] = jnp.where(lo, out0, out1).astype(o_ref.dtype)


def _attn_prompt(q, ktb, vtb, c4, *, batch, seq, blk=1024, wide=1, dsplit=1):
    t, d = q.shape
    npair = d // LANES
    nq = seq // blk
    qmap = lambda b, h, i: (b * nq + i, h)
    kmap = lambda b, h, i: (b, h, 0)
    return pl.pallas_call(
        functools.partial(_attn_prompt_kernel, blk=blk, wide=wide, dsplit=dsplit),
        grid=(batch, npair, nq),
        in_specs=[pl.BlockSpec((blk, LANES), qmap),
                  pl.BlockSpec((1, LANES, seq), kmap), pl.BlockSpec((1, LANES, seq), kmap),
                  pl.BlockSpec((1, 1, 2, seq), lambda b, h, i: (b, h, 0, 0))],
        out_specs=pl.BlockSpec((blk, LANES), qmap),
        out_shape=jax.ShapeDtypeStruct((t, d), BF16),
        scratch_shapes=[pltpu.VMEM((2, blk, LANES), F32), pltpu.VMEM((2, blk, LANES), F32)],
        compiler_params=_cparams(("parallel", "parallel", "arbitrary")),
        name="fox_attn_prompt",
    )(q, ktb, vtb, c4)


def _mlp_kernel(x_ref, a_ref, wo_ref, nw_ref, wup_ref, wdn_ref, fw_ref, o_ref, xn_sc, acc_sc, *, final_norm):
    f = pl.program_id(1)

    @pl.when(f == 0)
    def _():
        x = x_ref[...] + _dot(a_ref[...].astype(BF16), wo_ref[...])
        xn_sc[...] = _rmsnorm(x, nw_ref[...]).astype(BF16)
        acc_sc[...] = x

    h = jnp.maximum(_dot(xn_sc[...], wup_ref[...]), 0.0)
    acc_sc[...] += _dot((h * h).astype(BF16), wdn_ref[...])

    @pl.when(f == pl.num_programs(1) - 1)
    def _():
        y = acc_sc[...]
        if final_norm:
            y = _rmsnorm(y, fw_ref[...])
        o_ref[...] = y


def _mlp_block(x, a, wo, nw, wup, wdn, fw, *, final_norm, tm=1024, tf=1024):
    t, d = x.shape
    k = a.shape[1]
    dff = wup.shape[1]
    tm = min(tm, t)
    return pl.pallas_call(
        functools.partial(_mlp_kernel, final_norm=final_norm),
        grid=(t // tm, dff // tf),
        in_specs=[pl.BlockSpec((tm, d), lambda i, f: (i, 0)), pl.BlockSpec((tm, k), lambda i, f: (i, 0)),
                  pl.BlockSpec((k, d), lambda i, f: (0, 0)), pl.BlockSpec((1, d), lambda i, f: (0, 0)),
                  pl.BlockSpec((d, tf), lambda i, f: (0, f)), pl.BlockSpec((tf, d), lambda i, f: (f, 0)),
                  pl.BlockSpec((1, d), lambda i, f: (0, 0))],
        out_specs=pl.BlockSpec((tm, d), lambda i, f: (i, 0)),
        out_shape=jax.ShapeDtypeStruct((t, d), F32),
        scratch_shapes=[pltpu.VMEM((tm, d), BF16), pltpu.VMEM((tm, d), F32)],
        compiler_params=_cparams(("parallel", "arbitrary")),
        name="mlp_block",
    )(x, a, wo, nw, wup, wdn, fw)


def _norm_linear_kernel(x_ref, nw_ref, wt_ref, o_ref):
    xn = _rmsnorm(x_ref[...], nw_ref[...]).astype(BF16)
    o_ref[...] = _dot(xn, wt_ref[...], NT)


def _norm_linear(x, nw, wt, *, tm=128):
    t, d = x.shape
    n = wt.shape[0]
    return pl.pallas_call(
        _norm_linear_kernel,
        grid=(t // tm,),
        in_specs=[pl.BlockSpec((tm, d), lambda i: (i, 0)), pl.BlockSpec((1, d), lambda i: (0, 0)),
                  pl.BlockSpec((n, d), lambda i: (0, 0))],
        out_specs=pl.BlockSpec((tm, n), lambda i: (i, 0)),
        out_shape=jax.ShapeDtypeStruct((t, n), F32),
        compiler_params=_cparams(("parallel",)),
        name="norm_linear",
    )(x, nw, wt)


def _ssd_mixer_kernel(*refs, q, valid, nheads, ncol, pipe, preproj):
    nlead = 5 if preproj else 7
    (h0_ref, tx_ref, tbc_ref,
     cwx_ref, cwbc_ref, cbx_ref, cbbc_ref, arow_ref, acol_ref, drow_ref, nw_ref,
     ltri_ref, utri_ref,
     gy_ref, hout_ref, txo_ref, tbco_ref,
     h_sc, xpad_sc, bcpad_sc, z_sc, dtn_sc, dtt_sc, x_sc,
     sz_sc, sx_sc, sbc_sc, sdtn_sc, sdtt_sc) = refs[nlead:]
    c = pl.program_id(1)
    di = z_sc.shape[1]
    ngroups = di // SSD_GROUP_W
    first_scan = 1 if pipe else 0

    @pl.when(c == 0)
    def _():
        if pipe:
            for ref in (sz_sc, sx_sc, sbc_sc, sdtn_sc, sdtt_sc):
                ref[...] = jnp.zeros(ref.shape, F32)
        xpad_sc[0:8, :] = tx_ref[...]
        bcpad_sc[0:8, :] = tbc_ref[...]

    @pl.when(c <= first_scan)
    def _():
        h_sc[...] = h0_ref[...]

    if pipe:
        z_sc[...] = sz_sc[...]
        xpad_sc[8:8 + q, :] = sx_sc[...]
        bcpad_sc[8:8 + q, :] = sbc_sc[...]
        dtn_sc[...] = sdtn_sc[...]
        dtt_sc[...] = sdtt_sc[...]
        pz, px, pbc, pdtn, pdtt, prow = sz_sc, sx_sc, sbc_sc, sdtn_sc, sdtt_sc, 0
    else:
        pz, px, pbc, pdtn, pdtt, prow = z_sc, xpad_sc, bcpad_sc, dtn_sc, dtt_sc, 8

    if preproj:
        zin_ref, xin_ref, bcin_ref, dtnin_ref, dttin_ref = refs[:nlead]
        assert not pipe
        for dst, src, r0 in ((z_sc, zin_ref, 0), (xpad_sc, xin_ref, 8), (bcpad_sc, bcin_ref, 8),
                             (dtn_sc, dtnin_ref, 0)):
            if valid != q:
                dst[r0:r0 + q, :] = jnp.zeros((q, dst.shape[1]), F32)
            dst[r0:r0 + valid, :] = src[...]
        if valid != q:
            dtt_sc[...] = jnp.zeros(dtt_sc.shape, F32)
        dtt_sc[:, 0:valid] = dttin_ref[...]
    else:
        x_ref, nwx_ref, wzx_ref, wdtp_ref, wdt_ref, brow_ref, bcol_ref = refs[:nlead]
        if valid == q:
            x = x_ref[...]
        else:
            x_sc[...] = jnp.zeros(x_sc.shape, F32)
            x_sc[0:valid, :] = x_ref[...]
            x = x_sc[...]
        xn = _rmsnorm(x, nwx_ref[...]).astype(BF16)
        dtn_p = _softplus(_dot(xn, wdtp_ref[...], NT) + brow_ref[...])
        dtt_p = _softplus(_dot(wdt_ref[...], xn, NT) + bcol_ref[...])
        if valid != q:
            dtn_p = jnp.where(lax.broadcasted_iota(jnp.int32, dtn_p.shape, 0) < valid, dtn_p, 0.0)
            dtt_p = jnp.where(lax.broadcasted_iota(jnp.int32, dtt_p.shape, 1) < valid, dtt_p, 0.0)
        pdtn[...] = dtn_p
        pdtt[...] = dtt_p
        cw = (3 * di) // ncol

        def project(k):
            piece = _dot(xn, wzx_ref[k * cw:(k + 1) * cw, :], NT)
            tgt, off = divmod(k * cw, di)
            if tgt == 0:
                pz[:, off:off + cw] = piece
            elif tgt == 1:
                px[prow:prow + q, off:off + cw] = piece
            else:
                pbc[prow:prow + q, off:off + cw] = piece

        if not pipe:
            for k in range(ncol):
                project(k)

    def conv(pad_sc, w_ref, b_ref, sl):
        w = w_ref[:, sl]
        ext = pad_sc[:, sl]
        acc = b_ref[:, sl] + ext[8:8 + q] * w[3:4, :]
        for s in range(1, 4):
            acc = acc + pltpu.roll(ext, s, axis=0)[8:8 + q] * w[3 - s:4 - s, :]
        return _silu(acc)

    dtn = dtn_sc[...]
    dtt = dtt_sc[...]
    a_nat = dtn * arow_ref[...]
    a_t = dtt * acol_ref[...]
    cum_nat = _dot_f32_rhs(ltri_ref[...], a_nat)
    cum_t = _dot_f32_lhs(a_t, utri_ref[...])
    ri = lax.broadcasted_iota(jnp.int32, (q, q), 0)
    ci = lax.broadcasted_iota(jnp.int32, (q, q), 1)
    tri = ci <= ri
    lo = lax.broadcasted_iota(jnp.int32, (q, LANES), 1) < HEAD_DIM
    rlo = lax.broadcasted_iota(jnp.int32, (LANES, SSD_STATE), 0) < HEAD_DIM
    nb = di // 2
    clast_row = cum_nat[q - 1:q, :]
    e_nat = jnp.exp(cum_nat)
    w_nat = jnp.exp(clast_row - cum_nat) * dtn
    dl_row = jnp.exp(clast_row)

    for g in range(ngroups):
        if pipe:
            for k in range(g * ncol // ngroups, (g + 1) * ncol // ngroups):
                project(k)
        bg = conv(bcpad_sc, cwbc_ref, cbbc_ref, slice(g * SSD_STATE, (g + 1) * SSD_STATE))
        cg = conv(bcpad_sc, cwbc_ref, cbbc_ref, slice(nb + g * SSD_STATE, nb + (g + 1) * SSD_STATE))
        bb = bg.astype(BF16)
        cb16 = cg.astype(BF16)
        cb = _dot(cb16, bb, NT)
        gated = []
        ssq = jnp.zeros((q, 1), F32)
        for pr in range(2):
            p = 2 * g + pr
            sl = slice(p * LANES, (p + 1) * LANES)
            xp = conv(xpad_sc, cwx_ref, cbx_ref, sl)
            xpb = xp.astype(BF16)
            yi, ee, we, dl = [], [], [], []
            for r in range(2):
                h = 2 * p + r
                colb = jnp.broadcast_to(cum_nat[:, h:h + 1], (q, q))
                rowb = cum_t[h:h + 1, :]
                dec = jnp.exp(jnp.where(tri, colb - rowb, -jnp.inf))
                w = (cb * dec * dtt[h:h + 1, :]).astype(BF16)
                yi.append(_dot(w, xpb))
                ee.append(jnp.broadcast_to(e_nat[:, h:h + 1], (q, LANES)))
                we.append(jnp.broadcast_to(w_nat[:, h:h + 1], (q, LANES)))
                dl.append(jnp.broadcast_to(dl_row[:, h:h + 1], (LANES, SSD_STATE)))
            hp = h_sc[sl, :]
            y_inter = _dot(cb16, hp.astype(BF16), NT) * jnp.where(lo, ee[0], ee[1])
            xw = (xp * jnp.where(lo, we[0], we[1])).astype(BF16)
            h_sc[sl, :] = jnp.where(rlo, dl[0], dl[1]) * hp + _dot(xw, bb, TN)
            y = jnp.where(lo, yi[0], yi[1]) + y_inter + xp * drow_ref[:, sl]
            gt = y * _silu(z_sc[:, sl])
            ssq = ssq + jnp.sum(gt * gt, axis=-1, keepdims=True)
            gated.append(gt)
        rs = lax.rsqrt(ssq / SSD_GROUP_W + EPS)
        for pr in range(2):
            sl = slice((2 * g + pr) * LANES, (2 * g + pr + 1) * LANES)
            gy_ref[:, sl] = (gated[pr] * rs * nw_ref[:, sl])[0:valid].astype(gy_ref.dtype)

    @pl.when(c == pl.num_programs(1) - 1)
    def _():
        hout_ref[...] = h_sc[...]
        txo_ref[...] = xpad_sc[valid:valid + 8, :]
        tbco_ref[...] = bcpad_sc[valid:valid + 8, :]

    @pl.when(c >= first_scan)
    def _():
        xpad_sc[0:8, :] = xpad_sc[q:q + 8, :]
        bcpad_sc[0:8, :] = bcpad_sc[q:q + 8, :]


def _ssd_proj_kernel(x_ref, nw_ref, w_ref, wdtp_ref, wdt_ref, brow_ref, bcol_ref, zx_ref, dtn_ref, dtt_ref):
    xn = _rmsnorm(x_ref[...], nw_ref[...]).astype(BF16)
    dtn_ref[...] = _softplus(_dot(xn, wdtp_ref[...], NT) + brow_ref[...])
    dtt_ref[...] = _softplus(_dot(wdt_ref[...], xn, NT) + bcol_ref[...])
    zx_ref[...] = _dot(xn, w_ref[...], NT)


def _ssd_proj(x, nw, wt, wdtp, brow, bcol, *, tn=1024):
    t, d = x.shape
    nh = bcol.shape[0]
    n = wt.shape[0] - nh
    c0 = lambda j: (0, 0)
    return pl.pallas_call(
        _ssd_proj_kernel,
        grid=(n // tn,),
        in_specs=[pl.BlockSpec((t, d), c0), pl.BlockSpec((1, d), c0), pl.BlockSpec((tn, d), lambda j: (j, 0)),
                  pl.BlockSpec((LANES, d), c0), pl.BlockSpec((nh, d), lambda j: (n // nh, 0)),
                  pl.BlockSpec((1, LANES), c0), pl.BlockSpec((nh, 1), c0)],
        out_specs=[pl.BlockSpec((t, tn), lambda j: (0, j)), pl.BlockSpec((t, LANES), c0),
                   pl.BlockSpec((nh, t), c0)],
        out_shape=[jax.ShapeDtypeStruct((t, n), F32), jax.ShapeDtypeStruct((t, LANES), F32),
                   jax.ShapeDtypeStruct((nh, t), F32)],
        compiler_params=_cparams(("arbitrary",)),
        name="ssd_proj",
    )(x, nw, wt, wdtp, wt, brow, bcol)


def _ssd_mixer(x3, nwx, wt, wdtp, brow, bcol, h0, tail, cw, cb, arow, acol, drow, nw,
               *, chunk=256, short_chunk=16, ncol=6):
    b, seq, d = x3.shape
    q = chunk if seq >= chunk else short_chunk
    utri = _tri_upper(q)
    ltri = jnp.transpose(utri)
    nheads = bcol.shape[0]
    di = nheads * HEAD_DIM
    valid = min(q, seq)
    nc = max(seq // q, 1)
    pipe = nc > 1
    preproj = not pipe
    nsteps = nc + 1 if pipe else nc
    hrows = nheads * HEAD_DIM
    const2 = lambda k: (lambda i, c: (0, k))
    stage = [pltpu.VMEM((q, di), F32)] * 3 + [pltpu.VMEM((q, LANES), F32), pltpu.VMEM((nheads, q), F32)]
    if not pipe:
        stage = [pltpu.VMEM((8, LANES), F32)] * 5
    if pipe:
        ncol = 3 * (di // SSD_GROUP_W)
    if preproj:
        zx, dtn, dtt = _ssd_proj(x3.reshape(b * seq, d), nwx, wt, wdtp, brow, bcol)
        zx3 = zx.reshape(b, seq, 3 * di)
        lead = [zx3, zx3, zx3, dtn.reshape(b, seq, LANES),
                jnp.transpose(dtt.reshape(nheads, b, seq), (1, 0, 2))]
        lead_specs = [pl.BlockSpec((None, valid, di), lambda i, c: (i, c, 0)),
                      pl.BlockSpec((None, valid, di), lambda i, c: (i, c, 1)),
                      pl.BlockSpec((None, valid, di), lambda i, c: (i, c, 2)),
                      pl.BlockSpec((None, valid, LANES), lambda i, c: (i, c, 0)),
                      pl.BlockSpec((None, nheads, valid), lambda i, c: (i, 0, c))]
    else:
        lead = [x3, nwx, wt, wdtp, wt, brow, bcol]
        lead_specs = [pl.BlockSpec((None, valid, d), lambda i, c: (i, jnp.minimum(c, nc - 1), 0)),
                      pl.BlockSpec((1, d), const2(0)), pl.BlockSpec((3 * di, d), const2(0)),
                      pl.BlockSpec((LANES, d), const2(0)),
                      pl.BlockSpec((nheads, d), lambda i, c: ((3 * di) // nheads, 0)),
                      pl.BlockSpec((1, LANES), const2(0)), pl.BlockSpec((nheads, 1), const2(0))]
    kern = functools.partial(_ssd_mixer_kernel, q=q, valid=valid, nheads=nheads, ncol=ncol, pipe=pipe,
                             preproj=preproj)
    return pl.pallas_call(
        kern,
        grid=(b, nsteps),
        in_specs=lead_specs + [
                  pl.BlockSpec((None, hrows, SSD_STATE), lambda i, c: (i, 0, 0)),
                  pl.BlockSpec((None, 8, di), lambda i, c: (i, 0, 0)),
                  pl.BlockSpec((None, 8, di), lambda i, c: (i, 0, 1)),
                  pl.BlockSpec((4, di), const2(0)), pl.BlockSpec((4, di), const2(1)),
                  pl.BlockSpec((1, di), const2(0)), pl.BlockSpec((1, di), const2(1)),
                  pl.BlockSpec((1, LANES), const2(0)), pl.BlockSpec((nheads, 1), const2(0)),
                  pl.BlockSpec((1, di), const2(0)), pl.BlockSpec((1, di), const2(0)),
                  pl.BlockSpec((q, q), const2(0)), pl.BlockSpec((q, q), const2(0))],
        out_specs=[pl.BlockSpec((None, valid, di), lambda i, c: (i, jnp.maximum(c - (nsteps - nc), 0), 0)),
                   pl.BlockSpec((None, hrows, SSD_STATE), lambda i, c: (i, 0, 0)),
                   pl.BlockSpec((None, 8, di), lambda i, c: (i, 0, 0)),
                   pl.BlockSpec((None, 8, di), lambda i, c: (i, 0, 0))],
        out_shape=[jax.ShapeDtypeStruct((b, seq, di), BF16 if valid == q else F32),
                   jax.ShapeDtypeStruct((b, hrows, SSD_STATE), F32),
                   jax.ShapeDtypeStruct((b, 8, di), F32), jax.ShapeDtypeStruct((b, 8, di), F32)],
        scratch_shapes=[pltpu.VMEM((hrows, SSD_STATE), F32),
                        pltpu.VMEM((q + 8, di), F32), pltpu.VMEM((q + 8, di), F32),
                        pltpu.VMEM((q, di), F32), pltpu.VMEM((q, LANES), F32), pltpu.VMEM((nheads, q), F32),
                        pltpu.VMEM((q, d), F32)] + stage,
        compiler_params=_cparams(("parallel", "arbitrary")),
        name="ssd_mixer",
    )(*lead, h0, tail, tail, cw, cw, cb, cb, arow, acol, drow, nw, ltri, utri)


def _attn_sample_kernel(pt_ref, q_ref, kn_ref, vn_ref, f_ref, ft_ref, bfr_ref, bfc_ref, u_ref, hm_ref,
                        *refs, pg, nq, nh, scale):
    k_refs, v_refs, lf_refs = refs[0:pg], refs[pg:2 * pg], refs[2 * pg:3 * pg]
    o_ref, lfo_ref = refs[3 * pg], refs[3 * pg + 1]
    qbd_sc, m_sc, l_sc, acc_sc, carry_sc, knp_sc, vnp_sc = refs[3 * pg + 2:]
    g = pl.program_id(1)
    rows = nq * nh
    d = acc_sc.shape[1]

    @pl.when(g == 0)
    def _():
        qv = q_ref[...] * scale
        hm = hm_ref[...]
        for qi in range(nq):
            qbd_sc[qi * nh:(qi + 1) * nh, :] = (qv[qi:qi + 1, :] * hm).astype(BF16)
        m_sc[...] = jnp.full(m_sc.shape, -jnp.inf, F32)
        l_sc[...] = jnp.zeros(l_sc.shape, F32)
        acc_sc[...] = jnp.zeros(acc_sc.shape, F32)
        carry_sc[...] = jnp.zeros(carry_sc.shape, F32)

    def update(s, pv_fn):
        m_prev = m_sc[...]
        m_new = jnp.maximum(m_prev, jnp.max(s, axis=-1, keepdims=True))
        alpha = jnp.exp(m_prev - m_new)
        p = jnp.exp(s - jnp.concatenate([m_new] * (s.shape[1] // LANES), axis=1))
        l_sc[...] = alpha * l_sc[...] + jnp.sum(p, axis=-1, keepdims=True)
        acc_sc[...] = jnp.concatenate([alpha] * (d // LANES), axis=1) * acc_sc[...] + pv_fn(p.astype(BF16))
        m_sc[...] = m_new

    qbd = qbd_sc[...]
    off = carry_sc[...]
    scores = []
    for pi in range(pg):
        loc = _dot_f32_lhs(lf_refs[pi][...], u_ref[...])
        cpage = loc + off
        off = off + loc[:, LANES - 1:LANES]
        kt = k_refs[pi][...].astype(BF16)
        scores.append(_dot(qbd, kt) - jnp.concatenate([cpage] * nq, axis=0))
    carry_sc[...] = off

    def pv_pages(p):
        acc = None
        for pi in range(pg):
            part = _dot(p[:, pi * LANES:(pi + 1) * LANES], v_refs[pi][...].astype(BF16), NT)
            acc = part if acc is None else acc + part
        return acc

    update(jnp.concatenate(scores, axis=1), pv_pages)

    @pl.when(g == pl.num_programs(1) - 1)
    def _():
        lfo_ref[...] = _log_sigmoid(f_ref[:, 0:nh] + bfr_ref[...])
        lft = _log_sigmoid(ft_ref[...] + bfc_ref[...])
        lane = lax.broadcasted_iota(jnp.int32, (nh, LANES), 1)
        cnew = jnp.zeros((nh, LANES), F32)
        run = carry_sc[...]
        for t in range(nq):
            run = run + lft[:, t:t + 1]
            cnew = jnp.where(lane == t, run, cnew)
        knp_sc[...] = jnp.zeros(knp_sc.shape, F32)
        knp_sc[0:nq, :] = kn_ref[...]
        vnp_sc[...] = jnp.zeros(vnp_sc.shape, F32)
        vnp_sc[0:nq, :] = vn_ref[...]
        s = _dot(qbd_sc[...], knp_sc[...].astype(BF16), NT) - jnp.concatenate([cnew] * nq, axis=0)
        key = lax.broadcasted_iota(jnp.int32, (rows, LANES), 1)
        ridx = lax.broadcasted_iota(jnp.int32, (rows, LANES), 0)
        qrow = jnp.zeros((rows, LANES), jnp.int32)
        for qi in range(1, nq):
            qrow = qrow + (ridx >= qi * nh).astype(jnp.int32)
        s = jnp.where(key <= qrow, s, -jnp.inf)
        vnb = vnp_sc[...].astype(BF16)
        update(s, lambda p: _dot(p, vnb))
        out = acc_sc[...] / l_sc[:, 0:1]
        hm = hm_ref[...]
        for qi in range(nq):
            o_ref[qi:qi + 1, :] = jnp.sum(out[qi * nh:(qi + 1) * nh, :] * hm, axis=0,
                                          keepdims=True).astype(o_ref.dtype)


def _attn_sample(page_table, proj3, ft3, bf_row, bf_col, utri, hmask, kt_cache, vt_cache, lf_cache,
                 *, d, pg=16):
    nseq, nq, _ = proj3.shape
    npages = page_table.shape[1]
    nh = hmask.shape[0]
    rows = nq * nh
    scale = HEAD_DIM ** -0.5
    page = kt_cache.shape[-1]
    fblk = (3 * d) // LANES

    def pmap(pi):
        return lambda b, g, pt: (pt[b, g * pg + pi], 0, 0, 0)

    cmap = lambda b, g, pt: (0, 0)
    in_specs = [pl.BlockSpec((None, nq, d), lambda b, g, pt: (b, 0, 0)),
                pl.BlockSpec((None, nq, d), lambda b, g, pt: (b, 0, 1)),
                pl.BlockSpec((None, nq, d), lambda b, g, pt: (b, 0, 2)),
                pl.BlockSpec((None, nq, LANES), lambda b, g, pt: (b, 0, fblk)),
                pl.BlockSpec((None, nh, nq), lambda b, g, pt: (b, 0, 0)),
                pl.BlockSpec((1, nh), cmap), pl.BlockSpec((nh, 1), cmap),
                pl.BlockSpec((page, page), cmap), pl.BlockSpec((nh, d), cmap)]
    in_specs += [pl.BlockSpec((None, None, d, page), pmap(pi)) for pi in range(pg)]
    in_specs += [pl.BlockSpec((None, None, d, page), pmap(pi)) for pi in range(pg)]
    in_specs += [pl.BlockSpec((None, None, nh, page), pmap(pi)) for pi in range(pg)]
    grid_spec = pltpu.PrefetchScalarGridSpec(
        num_scalar_prefetch=1,
        grid=(nseq, npages // pg),
        in_specs=in_specs,
        out_specs=[pl.BlockSpec((None, nq, d), lambda b, g, pt: (b, 0, 0)),
                   pl.BlockSpec((None, nq, nh), lambda b, g, pt: (b, 0, 0))],
        scratch_shapes=[pltpu.VMEM((rows, d), BF16), pltpu.VMEM((rows, LANES), F32),
                        pltpu.VMEM((rows, LANES), F32),
                        pltpu.VMEM((rows, d), F32), pltpu.VMEM((nh, 1), F32),
                        pltpu.VMEM((LANES, d), F32), pltpu.VMEM((LANES, d), F32)])
    kern = functools.partial(_attn_sample_kernel, pg=pg, nq=nq, nh=nh, scale=scale)
    return pl.pallas_call(
        kern,
        grid_spec=grid_spec,
        out_shape=[jax.ShapeDtypeStruct((nseq, nq, d), F32), jax.ShapeDtypeStruct((nseq, nq, nh), F32)],
        compiler_params=_cparams(("parallel", "arbitrary")),
        name="fox_attn_sample",
    )(page_table, proj3, proj3, proj3, proj3, ft3, bf_row, bf_col, utri, hmask,
      *([kt_cache] * pg), *([vt_cache] * pg), *([lf_cache] * pg))


def _tri_upper(n):
    r = lax.broadcasted_iota(jnp.int32, (n, n), 0)
    c = lax.broadcasted_iota(jnp.int32, (n, n), 1)
    return (r <= c).astype(BF16)


def kernel(x_prompt, x_sample, cache_k, cache_v, cache_logf, page_table, state_ssm, state_conv,
           norm_mix_w, norm_mlp_w, norm_out_w, fox_w_in, fox_b_f, fox_w_out,
           ssd_w_in, ssd_conv_w, ssd_conv_b, ssd_dt_bias, ssd_a_log, ssd_d, ssd_norm_w, ssd_w_out,
           mlp_w_up, mlp_w_down):
    bp, lp, d = x_prompt.shape
    bs, ls, _ = x_sample.shape
    nh = fox_b_f.shape[-1]
    nsh = ssd_dt_bias.shape[-1]
    di = nsh * HEAD_DIM
    nzx = ssd_w_in.shape[-1] - nsh
    page = cache_k.shape[2]

    fox_wt = jnp.transpose(fox_w_in[0]).astype(BF16)
    npad = (-fox_wt.shape[0]) % LANES
    fox_wt_pad = jnp.concatenate([fox_wt, jnp.zeros((npad, d), BF16)], axis=0)
    bf_row = fox_b_f[0].reshape(1, nh)
    bf_col = fox_b_f[0].reshape(nh, 1)
    fox_wo = fox_w_out[0].astype(BF16)
    ssd_wt = jnp.transpose(ssd_w_in[0]).astype(BF16)
    wdtp = jnp.concatenate([ssd_wt[nzx:], jnp.zeros((LANES - nsh, d), BF16)], axis=0)
    dtb_row = jnp.concatenate([ssd_dt_bias[0], jnp.zeros((LANES - nsh,), F32)]).reshape(1, LANES)
    dtb_col = ssd_dt_bias[0].reshape(nsh, 1)
    a_neg = -jnp.exp(ssd_a_log[0])
    a_row = jnp.concatenate([a_neg, jnp.zeros((LANES - nsh,), F32)]).reshape(1, LANES)
    a_col = a_neg.reshape(nsh, 1)
    d_row = jnp.repeat(ssd_d[0], HEAD_DIM).reshape(1, di)
    ssd_nw = ssd_norm_w[0].reshape(1, di)
    ssd_wo = ssd_w_out[0].astype(BF16)
    conv_w = ssd_conv_w[0]
    conv_b = ssd_conv_b[0].reshape(1, -1)
    wup = [mlp_w_up[i].astype(BF16) for i in range(mlp_w_up.shape[0])]
    wdn = [mlp_w_down[i].astype(BF16) for i in range(mlp_w_down.shape[0])]
    nmix = norm_mix_w.reshape(-1, 1, d)
    nmlp = norm_mlp_w.reshape(-1, 1, d)
    nout = norm_out_w.reshape(1, d)
    utri = _tri_upper(LANES)
    utri_c = _tri_upper(256)
    hmask = (lax.broadcasted_iota(jnp.int32, (nh, d), 1) // HEAD_DIM
             == lax.broadcasted_iota(jnp.int32, (nh, d), 0)).astype(F32)

    def ssd_layer(x, batch, seq, h0, tail):
        gy, hlast, tx, tbc = _ssd_mixer(x.reshape(batch, seq, d), nmix[1], ssd_wt, wdtp, dtb_row, dtb_col,
                                        h0, tail, conv_w, conv_b, a_row, a_col, d_row, ssd_nw)
        y = _mlp_block(x, gy.reshape(batch * seq, di), ssd_wo, nmlp[1], wup[1], wdn[1], nout, final_norm=True)
        new_conv = jnp.concatenate([tx[:, 5:], tbc[:, 5:]], axis=-1).reshape(batch, 1, 3, nzx - di)
        return y, hlast.reshape(batch, 1, nsh, HEAD_DIM, SSD_STATE), new_conv

    tp = bp * lp
    xp = x_prompt.reshape(tp, d)
    q, kt, vt, ktb, vtb, lft = _fox_in_prompt(xp, nmix[0], fox_wt, bf_col, batch=bp, seq=lp)
    ct = _cumsum_lanes(lft, utri_c)
    o = _attn_prompt(q, ktb, vtb, ct.reshape(bp, nh // 2, 2, lp), batch=bp, seq=lp)
    xp = _mlp_block(xp, o, fox_wo, nmlp[0], wup[0], wdn[0], nout, final_norm=False)
    h0_p = jnp.zeros((bp, nsh * HEAD_DIM, SSD_STATE), state_ssm.dtype)
    tail_p = jnp.zeros((bp, 8, nzx - di), F32)
    y_p, ssm_p, conv_p = ssd_layer(xp, bp, lp, h0_p, tail_p)
    y_prompt = y_p.reshape(bp, lp, d)
    new_k_prompt = jnp.transpose(kt.reshape(bp, 1, nh, HEAD_DIM, lp), (0, 1, 4, 2, 3))
    new_v_prompt = jnp.transpose(vt.reshape(bp, 1, nh, HEAD_DIM, lp), (0, 1, 4, 2, 3))
    new_logf_prompt = jnp.transpose(lft.reshape(bp, 1, nh, lp), (0, 1, 3, 2))

    ts = bs * ls
    xs = x_sample.reshape(ts, d)
    proj = _norm_linear(xs, nmix[0], fox_wt_pad)
    proj3 = proj.reshape(bs, ls, -1)
    ft3 = jnp.transpose(proj3[:, :, 3 * d:3 * d + nh], (0, 2, 1))
    kt_cache = jnp.transpose(cache_k, (0, 1, 3, 4, 2)).reshape(cache_k.shape[0], cache_k.shape[1], d, page)
    vt_cache = jnp.transpose(cache_v, (0, 1, 3, 4, 2)).reshape(cache_v.shape[0], cache_v.shape[1], d, page)
    lf_cache = jnp.transpose(cache_logf, (0, 1, 3, 2))
    o_s, lf_s = _attn_sample(page_table, proj3, ft3, bf_row, bf_col, utri, hmask,
                             kt_cache, vt_cache, lf_cache, d=d)
    xs = _mlp_block(xs, o_s.reshape(ts, d), fox_wo, nmlp[0], wup[0], wdn[0], nout, final_norm=False)
    h0_s = state_ssm[:, 0].reshape(bs, nsh * HEAD_DIM, SSD_STATE)
    tail_s = jnp.concatenate([jnp.zeros((bs, 5, nzx - di), F32), state_conv[:, 0]], axis=1)
    y_s, ssm_s, conv_s = ssd_layer(xs, bs, ls, h0_s, tail_s)
    y_sample = y_s.reshape(bs, ls, d)
    new_k_sample = proj3[:, :, d:2 * d].reshape(bs, 1, ls, nh, HEAD_DIM)
    new_v_sample = proj3[:, :, 2 * d:3 * d].reshape(bs, 1, ls, nh, HEAD_DIM)
    new_logf_sample = lf_s.reshape(bs, 1, ls, nh)

    return (y_prompt, y_sample, new_k_prompt, new_v_prompt, new_logf_prompt, ssm_p, conv_p,
            new_k_sample, new_v_sample, new_logf_sample, ssm_s, conv_s)
```
